```python
import math
import jax, jax.numpy as jnp
from jax import lax
import numpy as np

D_MODEL = 1024
BATCH = 8
SEQ = 2048
DEPTH = 4
DEC_BATCH = 128
DEC_SEQ = 1
PAST_LEN = 2048
PAGE_SIZE = 128

N_MIXERS = 3
EPS = 1e-6
NEG = -1e30
TINY = 1e-30
D_FF = 2816
CONV_W = 3
HEAD_DIM = 64
NSA_HEADS = 16
NSA_GROUPS = 4
NSA_HPG = NSA_HEADS // NSA_GROUPS
CMP_BLOCK = 32
SEL_BLOCK = 64
CMP_PER_SEL = SEL_BLOCK // CMP_BLOCK
N_SEL = 8
WINDOW = 512
NSA_QB = 64
FORCE_SCORE = 1e4
NSA_QW = NSA_HEADS * HEAD_DIM
NSA_KVW = NSA_GROUPS * 6 * HEAD_DIM
NSA_IN = NSA_QW + NSA_KVW + NSA_HEADS * 3
DIFF_HEADS = 8
DIFF_KV_HEADS = 4
DIFF_REP = DIFF_HEADS // DIFF_KV_HEADS
DIFF_QW = DIFF_HEADS * 2 * HEAD_DIM
DIFF_IN = DIFF_QW + DIFF_KV_HEADS * 4 * HEAD_DIM
DIFF_QB = 128
N_CONV = (DEPTH + 2) // 3
N_NSA = (DEPTH + 1) // 3
N_DIFF = DEPTH // 3

kernel_name = 'hybrid_conv_nsa_diffattn_decoder_step'


def rms_norm(x, g):
    xf = x.astype(jnp.float32)
    y = xf * lax.rsqrt(jnp.mean(xf * xf, axis=-1, keepdims=True) + EPS)
    return (y * g.astype(jnp.float32)).astype(x.dtype)


def masked_softmax(s, mask):
    s = jnp.where(mask, s.astype(jnp.float32), NEG)
    m = jnp.max(s, axis=-1, keepdims=True)
    p = jnp.where(mask, jnp.exp(s - m), 0.0)
    return p / jnp.maximum(jnp.sum(p, axis=-1, keepdims=True), TINY)


def half_ffn(x, g_pre, g_post, w_in, w_out):
    gate, up = jnp.split(rms_norm(x, g_pre) @ w_in, 2, axis=-1)
    return x + 0.5 * rms_norm((jax.nn.silu(gate) * up) @ w_out, g_post)


def short_conv(h, buf, w_in, w_conv, w_out):
    T = h.shape[1]
    c, b, v = jnp.split(h @ w_in, 3, axis=-1)
    u = jnp.concatenate([buf.astype(h.dtype), c * v], axis=1)
    y = w_conv[0] * u[:, 0:T]
    for j in range(1, CONV_W):
        y = y + w_conv[j] * u[:, j:j + T]
    return (b * y) @ w_out, u[:, T:]


def nsa_project(h, w_in):
    B, T, _ = h.shape
    p = h @ w_in
    q = p[..., :NSA_QW].reshape(B, T, NSA_GROUPS, NSA_HPG, HEAD_DIM)
    kv = p[..., NSA_QW:NSA_QW + NSA_KVW].reshape(B, T, NSA_GROUPS, 6 * HEAD_DIM)
    gates = jax.nn.sigmoid(p[..., NSA_QW + NSA_KVW:]).reshape(B, T, NSA_GROUPS, NSA_HPG, 3)
    return q, kv, gates


def nsa_blocks(rows, w_cmp):
    B, L, G, _ = rows.shape
    n_sel = -(-L // SEL_BLOCK)
    n_cmp = n_sel * CMP_PER_SEL
    rows = jnp.pad(rows, ((0, 0), (0, n_sel * SEL_BLOCK - L), (0, 0), (0, 0)))
    k_c, v_c, k_s, v_s = jnp.split(rows, 4, axis=-1)
    kc = jnp.einsum('bnlgd,ld->bngd', k_c.reshape(B, n_cmp, CMP_BLOCK, G, HEAD_DIM), w_cmp[0])
    vc = jnp.einsum('bnlgd,ld->bngd', v_c.reshape(B, n_cmp, CMP_BLOCK, G, HEAD_DIM), w_cmp[1])
    ks = k_s.reshape(B, n_sel, SEL_BLOCK, G, HEAD_DIM).transpose(0, 3, 1, 2, 4)
    vs = v_s.reshape(B, n_sel, SEL_BLOCK, G, HEAD_DIM).transpose(0, 3, 1, 2, 4)
    return kc, vc, ks, vs


def nsa_attend(q, gates, q_pos, blocks, win, w_pos):
    kc, vc, ks, vs = blocks
    B, Tq = q.shape[:2]
    scale = HEAD_DIM ** -0.5
    n_cmp = kc.shape[1]
    cmp_end = (jnp.arange(n_cmp, dtype=jnp.int32) + 1) * CMP_BLOCK - 1
    mask_c = (cmp_end[None, :] <= q_pos[:, None])[None, :, None, None, :]
    p_c = masked_softmax(jnp.einsum('bqghd,bngd->bqghn', q, kc) * scale, mask_c)
    o_c = jnp.einsum('bqghn,bngd->bqghd', p_c.astype(vc.dtype), vc)
    n_sel = ks.shape[2]
    imp = p_c.sum(axis=3).reshape(B, Tq, NSA_GROUPS, n_sel, CMP_PER_SEL).sum(-1)
    blk = jnp.arange(n_sel, dtype=jnp.int32)
    forced = ((q_pos[:, None] // SEL_BLOCK) == blk[None, :]) | (blk == 0)[None, :]
    valid = (blk * SEL_BLOCK)[None, :] <= q_pos[:, None]
    imp = jnp.where(forced[None, :, None, :], FORCE_SCORE,
                    jnp.where(valid[None, :, None, :], imp, NEG))
    k_top = min(N_SEL, n_sel)
    _, idx = lax.top_k(imp, k_top)
    idx = idx.transpose(0, 2, 1, 3)
    bi = jnp.arange(B)[:, None, None, None]
    gi = jnp.arange(NSA_GROUPS)[None, :, None, None]
    k_g = ks[bi, gi, idx]
    v_g = vs[bi, gi, idx].reshape(B, NSA_GROUPS, Tq, k_top * SEL_BLOCK, HEAD_DIM)
    key_pos = idx[..., None] * SEL_BLOCK + jnp.arange(SEL_BLOCK, dtype=jnp.int32)
    mask_s = (key_pos <= q_pos[None, None, :, None, None]).transpose(0, 2, 1, 3, 4)
    mask_s = mask_s.reshape(B, Tq, NSA_GROUPS, 1, k_top * SEL_BLOCK)
    s_s = jnp.einsum('bqghd,bgqkld->bqghkl', q, k_g) * scale
    p_s = masked_softmax(s_s.reshape(B, Tq, NSA_GROUPS, NSA_HPG, k_top * SEL_BLOCK), mask_s)
    o_s = jnp.einsum('bqghn,bgqnd->bqghd', p_s.astype(v_g.dtype), v_g)
    kw, vw = jnp.split(win, 2, axis=-1)
    mask_w = ((w_pos[None, :] <= q_pos[:, None]) & (w_pos[None, :] > q_pos[:, None] - WINDOW)
              & (w_pos[None, :] >= 0))[None, :, None, None, :]
    p_w = masked_softmax(jnp.einsum('bqghd,bkgd->bqghk', q, kw) * scale, mask_w)
    o_w = jnp.einsum('bqghk,bkgd->bqghd', p_w.astype(vw.dtype), vw)
    return gates[..., 0:1] * o_c + gates[..., 1:2] * o_s + gates[..., 2:3] * o_w


def nsa_prompt(h, w_in, w_cmp, w_out):
    B, T, _ = h.shape
    q, kv, gates = nsa_project(h, w_in)
    rows, win = kv[..., :4 * HEAD_DIM], kv[..., 4 * HEAD_DIM:]
    blocks = nsa_blocks(rows, w_cmp)
    win_pad = jnp.pad(win, ((0, 0), (WINDOW, 0), (0, 0), (0, 0)))

    def chunk(ci):
        c0 = ci * NSA_QB
        qc = lax.dynamic_slice_in_dim(q, c0, NSA_QB, axis=1)
        gc = lax.dynamic_slice_in_dim(gates, c0, NSA_QB, axis=1)
        wc = lax.dynamic_slice_in_dim(win_pad, c0, WINDOW + NSA_QB, axis=1)
        q_pos = c0 + jnp.arange(NSA_QB, dtype=jnp.int32)
        w_pos = c0 - WINDOW + jnp.arange(WINDOW + NSA_QB, dtype=jnp.int32)
        return nsa_attend(qc, gc, q_pos, blocks, wc, w_pos)

    o = lax.map(chunk, jnp.arange(T // NSA_QB, dtype=jnp.int32))
    o = jnp.moveaxis(o, 0, 1).reshape(B, T, NSA_QW)
    return o @ w_out, rows, win[:, T - min(WINDOW, T):]


def nsa_sample(h, cache, win_buf, page_table, w_in, w_cmp, w_out):
    B, T, _ = h.shape
    q, kv, gates = nsa_project(h, w_in)
    rows, win_new = kv[..., :4 * HEAD_DIM], kv[..., 4 * HEAD_DIM:]
    past_len = page_table.shape[1] * cache.shape[1]
    past = cache[page_table].reshape(B, past_len, NSA_GROUPS, 4 * HEAD_DIM).astype(rows.dtype)
    blocks = nsa_blocks(jnp.concatenate([past, rows], axis=1), w_cmp)
    w_buf = win_buf.shape[1]
    win = jnp.concatenate([win_buf.astype(win_new.dtype), win_new], axis=1)
    q_pos = past_len + jnp.arange(T, dtype=jnp.int32)
    w_pos = past_len - w_buf + jnp.arange(w_buf + T, dtype=jnp.int32)
    o = nsa_attend(q, gates, q_pos, blocks, win, w_pos)
    return o.reshape(B, T, NSA_QW) @ w_out, rows, win[:, T:]


def diff_project(h, w_in):
    B, T, _ = h.shape
    p = h @ w_in
    q = p[..., :DIFF_QW].reshape(B, T, DIFF_KV_HEADS, DIFF_REP, 2, HEAD_DIM)
    kv = p[..., DIFF_QW:].reshape(B, T, DIFF_KV_HEADS, 4 * HEAD_DIM)
    return q, kv


def diff_lambda_value(lam_p, lam_init):
    lp = lam_p.astype(jnp.float32)
    return jnp.exp(jnp.sum(lp[0] * lp[1])) - jnp.exp(jnp.sum(lp[2] * lp[3])) + lam_init


def diff_attend(q, q_pos, kv, k_pos, lam):
    B, Tk = kv.shape[:2]
    k = kv[..., :2 * HEAD_DIM].reshape(B, Tk, DIFF_KV_HEADS, 2, HEAD_DIM)
    v = kv[..., 2 * HEAD_DIM:]
    s = jnp.einsum('bqgrmd,bkgmd->bgrmqk', q, k) * HEAD_DIM ** -0.5
    p = masked_softmax(s, k_pos[None, :] <= q_pos[:, None])
    a = (p[:, :, :, 0] - lam * p[:, :, :, 1]).astype(v.dtype)
    return jnp.einsum('bgrqk,bkgd->bqgrd', a, v)


def diff_output(o, lam_init, subln_g, w_out):
    B, T = o.shape[:2]
    o = rms_norm(o, subln_g) * (1.0 - lam_init)
    return o.reshape(B, T, DIFF_HEADS * 2 * HEAD_DIM) @ w_out


def diff_prompt(h, w_in, lam_p, lam_init, subln_g, w_out):
    B, T, _ = h.shape
    q, kv = diff_project(h, w_in)
    lam = diff_lambda_value(lam_p, lam_init)
    k_pos = jnp.arange(T, dtype=jnp.int32)

    def block(ci):
        c0 = ci * DIFF_QB
        qb = lax.dynamic_slice_in_dim(q, c0, DIFF_QB, axis=1)
        return diff_attend(qb, c0 + jnp.arange(DIFF_QB, dtype=jnp.int32), kv, k_pos, lam)

    o = lax.map(block, jnp.arange(T // DIFF_QB, dtype=jnp.int32))
    o = jnp.moveaxis(o, 0, 1).reshape(B, T, DIFF_KV_HEADS, DIFF_REP, 2 * HEAD_DIM)
    return diff_output(o, lam_init, subln_g, w_out), kv


def diff_sample(h, cache, page_table, w_in, lam_p, lam_init, subln_g, w_out):
    B, T, _ = h.shape
    q, kv = diff_project(h, w_in)
    past_len = page_table.shape[1] * cache.shape[1]
    past = cache[page_table].reshape(B, past_len, DIFF_KV_HEADS, 4 * HEAD_DIM).astype(kv.dtype)
    keys = jnp.concatenate([past, kv], axis=1)
    k_pos = jnp.arange(past_len + T, dtype=jnp.int32)
    q_pos = past_len + jnp.arange(T, dtype=jnp.int32)
    o = diff_attend(q, q_pos, keys, k_pos, diff_lambda_value(lam_p, lam_init))
    return diff_output(o, lam_init, subln_g, w_out), kv


def setup_inputs(seed: int = 0) -> dict:
    key = jax.random.key(seed)
    ks = jax.random.split(key, 24)
    f32 = jnp.float32

    def nrm(k, shape, scale):
        return jax.random.normal(k, shape, f32) * scale

    n_pages = PAST_LEN // PAGE_SIZE
    n_used = DEC_BATCH * n_pages
    n_phys = n_used + (n_used + 3) // 4
    w_buf = min(WINDOW, PAST_LEN)
    page_table = jax.random.permutation(ks[6], n_phys)[:n_used].reshape(DEC_BATCH, n_pages).astype(jnp.int32)
    return {
        'x_prompt': nrm(ks[0], (BATCH, SEQ, D_MODEL), 1.0),
        'x_sample': nrm(ks[1], (DEC_BATCH, DEC_SEQ, D_MODEL), 1.0),
        'state_conv': nrm(ks[2], (N_CONV, DEC_BATCH, CONV_W - 1, D_MODEL), 1.0),
        'cache_nsa': nrm(ks[3], (N_NSA, n_phys, PAGE_SIZE, NSA_GROUPS, 4 * HEAD_DIM), 1.0),
        'state_nsa_win': nrm(ks[4], (N_NSA, DEC_BATCH, w_buf, NSA_GROUPS, 2 * HEAD_DIM), 1.0),
        'cache_diff': nrm(ks[5], (N_DIFF, n_phys, PAGE_SIZE, DIFF_KV_HEADS, 4 * HEAD_DIM), 1.0),
        'page_table': page_table,
        'norm_g': 1.0 + nrm(ks[7], (DEPTH, 6, D_MODEL), 0.02),
        'ffn_w_in': nrm(ks[8], (DEPTH, 2, D_MODEL, 2 * D_FF), D_MODEL ** -0.5),
        'ffn_w_out': nrm(ks[9], (DEPTH, 2, D_FF, D_MODEL), D_FF ** -0.5),
        'conv_w_in': nrm(ks[10], (N_CONV, D_MODEL, 3 * D_MODEL), D_MODEL ** -0.5),
        'conv_w': nrm(ks[11], (N_CONV, CONV_W, D_MODEL), CONV_W ** -0.5),
        'conv_w_out': nrm(ks[12], (N_CONV, D_MODEL, D_MODEL), D_MODEL ** -0.5),
        'nsa_w_in': nrm(ks[13], (N_NSA, D_MODEL, NSA_IN), D_MODEL ** -0.5),
        'nsa_w_cmp': nrm(ks[14], (N_NSA, 2, CMP_BLOCK, HEAD_DIM), CMP_BLOCK ** -0.5),
        'nsa_w_out': nrm(ks[15], (N_NSA, NSA_QW, D_MODEL), NSA_QW ** -0.5),
        'diff_w_in': nrm(ks[16], (N_DIFF, D_MODEL, DIFF_IN), D_MODEL ** -0.5),
        'diff_lambda': nrm(ks[17], (N_DIFF, 4, HEAD_DIM), 0.1),
        'diff_subln_g': 1.0 + nrm(ks[18], (N_DIFF, 2 * HEAD_DIM), 0.02),
        'diff_w_out': nrm(ks[19], (N_DIFF, DIFF_QW, D_MODEL), DIFF_QW ** -0.5),
    }


def reference(x_prompt, x_sample, state_conv, cache_nsa, state_nsa_win, cache_diff, page_table,
              norm_g, ffn_w_in, ffn_w_out, conv_w_in, conv_w, conv_w_out,
              nsa_w_in, nsa_w_cmp, nsa_w_out, diff_w_in, diff_lambda, diff_subln_g, diff_w_out):
    xp, xs = x_prompt, x_sample
    conv_p, conv_s, nsa_p, nsa_s, win_p, win_s, diff_p, diff_s = [], [], [], [], [], [], [], []
    for i in range(DEPTH):
        g = norm_g[i]
        kind, j = i % N_MIXERS, i // N_MIXERS
        xp = half_ffn(xp, g[0], g[1], ffn_w_in[i, 0], ffn_w_out[i, 0])
        xs = half_ffn(xs, g[0], g[1], ffn_w_in[i, 0], ffn_w_out[i, 0])
        hp, hs = rms_norm(xp, g[2]), rms_norm(xs, g[2])
        if kind == 0:
            zero_buf = jnp.zeros((hp.shape[0], CONV_W - 1, hp.shape[2]), hp.dtype)
            yp, st = short_conv(hp, zero_buf, conv_w_in[j], conv_w[j], conv_w_out[j])
            conv_p.append(st)
            ys, st = short_conv(hs, state_conv[j], conv_w_in[j], conv_w[j], conv_w_out[j])
            conv_s.append(st)
        elif kind == 1:
            yp, rows, win = nsa_prompt(hp, nsa_w_in[j], nsa_w_cmp[j], nsa_w_out[j])
            nsa_p.append(rows)
            win_p.append(win)
            ys, rows, win = nsa_sample(hs, cache_nsa[j], state_nsa_win[j], page_table,
                                       nsa_w_in[j], nsa_w_cmp[j], nsa_w_out[j])
            nsa_s.append(rows)
            win_s.append(win)
        else:
            lam_init = 0.8 - 0.6 * math.exp(-0.3 * i)
            yp, rows = diff_prompt(hp, diff_w_in[j], diff_lambda[j], lam_init, diff_subln_g[j], diff_w_out[j])
            diff_p.append(rows)
            ys, rows = diff_sample(hs, cache_diff[j], page_table, diff_w_in[j], diff_lambda[j],
                                   lam_init, diff_subln_g[j], diff_w_out[j])
            diff_s.append(rows)
        xp = xp + rms_norm(yp, g[3])
        xs = xs + rms_norm(ys, g[3])
        xp = half_ffn(xp, g[4], g[5], ffn_w_in[i, 1], ffn_w_out[i, 1])
        xs = half_ffn(xs, g[4], g[5], ffn_w_in[i, 1], ffn_w_out[i, 1])
    new_conv_prompt = jnp.stack(conv_p)
    new_conv_sample = jnp.stack(conv_s)
    new_nsa_kv_prompt = jnp.stack(nsa_p)
    new_nsa_kv_sample = jnp.stack(nsa_s)
    new_nsa_win_prompt = jnp.stack(win_p)
    new_nsa_win_sample = jnp.stack(win_s)
    new_diff_kv_prompt = jnp.stack(diff_p)
    new_diff_kv_sample = jnp.stack(diff_s)
    return (xp, xs, new_conv_prompt, new_conv_sample, new_nsa_kv_prompt, new_nsa_kv_sample,
            new_nsa_win_prompt, new_nsa_win_sample, new_diff_kv_prompt, new_diff_kv_sample)
```

```python
import functools
import math

import jax
import jax.numpy as jnp
from jax import lax
from jax.experimental import pallas as pl
from jax.experimental.pallas import tpu as pltpu

F32 = jnp.float32
BF16 = jnp.bfloat16

EPS = 1e-6
NEG = -1e30
TINY = 1e-30
N_MIXERS = 3
CONV_W = 3
HEAD_DIM = 64
NSA_GROUPS = 4
NSA_HPG = 4
CMP_BLOCK = 32
SEL_BLOCK = 64
N_SEL = 8
WINDOW = 512
FORCE_SCORE = 1e4
DIFF_KV_HEADS = 4
DIFF_REP = 2
QK_SCALE = HEAD_DIM ** -0.5

LANES = 128
VMEM_LIMIT_BYTES = 56 * 1024 * 1024

_NT = (((1,), (1,)), ((), ()))


def _params(*sem):
    return pltpu.CompilerParams(dimension_semantics=sem, vmem_limit_bytes=VMEM_LIMIT_BYTES)


def _rms(x, g):
    return x * lax.rsqrt(jnp.mean(x * x, axis=-1, keepdims=True) + EPS) * g


def _dot(a, b):
    return jnp.dot(a, b, preferred_element_type=F32)


def _dot_nt(a, b):
    return lax.dot_general(a, b, _NT, preferred_element_type=F32)


def _masked_softmax(s, mask):
    s = jnp.where(mask, s, NEG)
    m = jnp.max(s, axis=-1, keepdims=True)
    p = jnp.where(mask, jnp.exp(s - m), 0.0)
    return p / jnp.maximum(jnp.sum(p, axis=-1, keepdims=True), TINY)


def _low_half(x):
    lane = lax.broadcasted_iota(jnp.int32, x.shape, 1)
    return jnp.where(lane < HEAD_DIM, x, 0.0)


def _high_half(x):
    lane = lax.broadcasted_iota(jnp.int32, x.shape, 1)
    return jnp.where(lane >= HEAD_DIM, x, 0.0)


def _swap_halves(x):
    return pltpu.roll(x, HEAD_DIM, 1)


def _ffn_kernel(x_ref, gpre_ref, gpost_ref, wg_ref, wu_ref, wo_ref, o_ref, xn_ref, acc_ref):
    j = pl.program_id(1)

    @pl.when(j == 0)
    def _():
        xn_ref[...] = _rms(x_ref[...], gpre_ref[...]).astype(BF16)

    xn = xn_ref[...]
    gate = _dot(xn, wg_ref[...])
    up = _dot(xn, wu_ref[...])
    act = (gate * (1.0 / (1.0 + jnp.exp(-gate))) * up).astype(BF16)
    part = _dot(act, wo_ref[...])

    @pl.when(j == 0)
    def _():
        acc_ref[...] = part

    @pl.when(j > 0)
    def _():
        acc_ref[...] += part

    @pl.when(j == pl.num_programs(1) - 1)
    def _():
        o_ref[...] = x_ref[...] + 0.5 * _rms(acc_ref[...], gpost_ref[...])


def _ffn(x, g_pre, g_post, w_in, w_out, tm):
    m, d = x.shape
    f = w_out.shape[0]
    tf = f // 2
    nf = f // tf
    return pl.pallas_call(
        _ffn_kernel,
        grid=(m // tm, nf),
        in_specs=[
            pl.BlockSpec((tm, d), lambda i, j: (i, 0)),
            pl.BlockSpec((1, d), lambda i, j: (0, 0)),
            pl.BlockSpec((1, d), lambda i, j: (0, 0)),
            pl.BlockSpec((d, tf), lambda i, j: (0, j)),
            pl.BlockSpec((d, tf), lambda i, j: (0, j + nf)),
            pl.BlockSpec((tf, d), lambda i, j: (j, 0)),
        ],
        out_specs=pl.BlockSpec((tm, d), lambda i, j: (i, 0)),
        out_shape=jax.ShapeDtypeStruct((m, d), F32),
        scratch_shapes=[pltpu.VMEM((tm, d), BF16), pltpu.VMEM((tm, d), F32)],
        compiler_params=_params("arbitrary", "arbitrary"),
        name="half_ffn",
    )(x, g_pre, g_post, w_in, w_in, w_out)


def _post_kernel(x_ref, o_ref, w_ref, g_ref, y_ref):
    y = _dot(o_ref[...], w_ref[...])
    y_ref[...] = x_ref[...] + _rms(y, g_ref[...])


def _post(x, o, w, g, tm):
    m, d = x.shape
    k = o.shape[1]
    return pl.pallas_call(
        _post_kernel,
        grid=(m // tm,),
        in_specs=[
            pl.BlockSpec((tm, d), lambda i: (i, 0)),
            pl.BlockSpec((tm, k), lambda i: (i, 0)),
            pl.BlockSpec((k, d), lambda i: (0, 0)),
            pl.BlockSpec((1, d), lambda i: (0, 0)),
        ],
        out_specs=pl.BlockSpec((tm, d), lambda i: (i, 0)),
        out_shape=jax.ShapeDtypeStruct((m, d), F32),
        compiler_params=_params("arbitrary"),
        name="mixer_out",
    )(x, o, w, g)


def _conv_prompt_kernel(x_ref, g2_ref, g3_ref, win_ref, cw_ref, wout_ref, y_ref, st_ref, ubuf_ref):
    t = pl.program_id(1)
    tm, d = x_ref.shape
    x = x_ref[...]
    h = _rms(x, g2_ref[...]).astype(BF16)
    p = _dot(h, win_ref[...])
    c, b, v = p[:, :d], p[:, d:2 * d], p[:, 2 * d:]
    u = c * v

    @pl.when(t == 0)
    def _():
        ubuf_ref[0:8, :] = jnp.zeros((8, d), F32)

    ubuf_ref[8:8 + tm, :] = u
    cw = cw_ref[...]
    y = cw[0:1] * ubuf_ref[6:6 + tm, :] + cw[1:2] * ubuf_ref[7:7 + tm, :] + cw[2:3] * u
    out = _dot((b * y).astype(BF16), wout_ref[...])
    y_ref[...] = x + _rms(out, g3_ref[...])
    last2 = ubuf_ref[6 + tm:8 + tm, :]
    st_ref[...] = last2
    ubuf_ref[6:8, :] = last2


def _conv_prompt(x, g2, g3, w_in, cw, w_out, batch, tm):
    m, d = x.shape
    t = m // batch
    nt = t // tm
    return pl.pallas_call(
        _conv_prompt_kernel,
        grid=(batch, nt),
        in_specs=[
            pl.BlockSpec((tm, d), lambda b, i: (b * nt + i, 0)),
            pl.BlockSpec((1, d), lambda b, i: (0, 0)),
            pl.BlockSpec((1, d), lambda b, i: (0, 0)),
            pl.BlockSpec((d, 3 * d), lambda b, i: (0, 0)),
            pl.BlockSpec((CONV_W, d), lambda b, i: (0, 0)),
            pl.BlockSpec((d, d), lambda b, i: (0, 0)),
        ],
        out_specs=[
            pl.BlockSpec((tm, d), lambda b, i: (b * nt + i, 0)),
            pl.BlockSpec((None, CONV_W - 1, d), lambda b, i: (b, 0, 0)),
        ],
        out_shape=[
            jax.ShapeDtypeStruct((m, d), F32),
            jax.ShapeDtypeStruct((batch, CONV_W - 1, d), F32),
        ],
        scratch_shapes=[pltpu.VMEM((tm + 8, d), F32)],
        compiler_params=_params("arbitrary", "arbitrary"),
        name="conv_prompt",
    )(x, g2, g3, w_in, cw, w_out)


def _conv_sample_kernel(x_ref, st_ref, g2_ref, g3_ref, win_ref, cw_ref, wout_ref, y_ref, nst_ref):
    d = x_ref.shape[1]
    x = x_ref[...]
    h = _rms(x, g2_ref[...]).astype(BF16)
    p = _dot(h, win_ref[...])
    c, b, v = p[:, :d], p[:, d:2 * d], p[:, 2 * d:]
    u = c * v
    s0, s1 = st_ref[:, :d], st_ref[:, d:]
    cw = cw_ref[...]
    y = cw[0:1] * s0 + cw[1:2] * s1 + cw[2:3] * u
    out = _dot((b * y).astype(BF16), wout_ref[...])
    y_ref[...] = x + _rms(out, g3_ref[...])
    nst_ref[:, :d] = s1
    nst_ref[:, d:] = u


def _conv_sample(x, state, g2, g3, w_in, cw, w_out):
    m, d = x.shape
    return pl.pallas_call(
        _conv_sample_kernel,
        out_shape=[
            jax.ShapeDtypeStruct((m, d), F32),
            jax.ShapeDtypeStruct((m, (CONV_W - 1) * d), F32),
        ],
        compiler_params=pltpu.CompilerParams(vmem_limit_bytes=VMEM_LIMIT_BYTES),
        name="conv_sample",
    )(x, state, g2, g3, w_in, cw, w_out)


def _diff_proj_kernel(x_ref, g_ref, w_ref, q_ref, kv_ref, kvb_ref):
    h = _rms(x_ref[...], g_ref[...]).astype(BF16)
    p = _dot(h, w_ref[...])
    nq = q_ref.shape[1]
    q_ref[...] = p[:, :nq] * QK_SCALE
    kv = p[:, nq:]
    kv_ref[...] = kv
    kvb_ref[...] = kv.astype(BF16)


def _diff_proj(x, g, w, tm):
    m, d = x.shape
    n = w.shape[1]
    nq = DIFF_KV_HEADS * DIFF_REP * 2 * HEAD_DIM
    nkv = n - nq
    return pl.pallas_call(
        _diff_proj_kernel,
        grid=(m // tm,),
        in_specs=[
            pl.BlockSpec((tm, d), lambda i: (i, 0)),
            pl.BlockSpec((1, d), lambda i: (0, 0)),
            pl.BlockSpec((d, n), lambda i: (0, 0)),
        ],
        out_specs=[
            pl.BlockSpec((tm, nq), lambda i: (i, 0)),
            pl.BlockSpec((tm, nkv), lambda i: (i, 0)),
            pl.BlockSpec((tm, nkv), lambda i: (i, 0)),
        ],
        out_shape=[
            jax.ShapeDtypeStruct((m, nq), F32),
            jax.ShapeDtypeStruct((m, nkv), F32),
            jax.ShapeDtypeStruct((m, nkv), BF16),
        ],
        compiler_params=_params("arbitrary"),
        name="diff_proj",
    )(x, g, w)


def _diff_lambda(lam_ref, lam_init):
    lp = lam_ref[...]
    a = jnp.sum(lp[0:1] * lp[1:2], axis=-1, keepdims=True)
    b = jnp.sum(lp[2:3] * lp[3:4], axis=-1, keepdims=True)
    return jnp.exp(a) - jnp.exp(b) + lam_init


def _diff_finish(o1, o2, lam, sub_g, lam_init):
    dlt = o1 - lam * o2
    return _rms(dlt, sub_g) * (1.0 - lam_init)


def _online_step(q, kv_k, kv_v, mask, m_prev, l_prev, acc_prev):
    s = jnp.where(mask, _dot_nt(q, kv_k), NEG)
    m_next = jnp.maximum(m_prev, jnp.max(s, axis=-1, keepdims=True))
    p = jnp.where(mask, jnp.exp(s - m_next), 0.0)
    alpha = jnp.exp(m_prev - m_next)
    l_next = alpha * l_prev + jnp.sum(p, axis=-1, keepdims=True)
    acc_next = alpha * acc_prev + _dot(p.astype(BF16), kv_v)
    return m_next, l_next, acc_next


def _online_init(rows):
    return (jnp.full((rows, 1), NEG, F32), jnp.zeros((rows, 1), F32), jnp.zeros((rows, LANES), F32))


def _tile4(x):
    return jnp.concatenate([x] * 4, axis=0)


def _stacked_positions(first, rows):
    r = lax.broadcasted_iota(jnp.int32, (4 * rows, 1), 0)
    return first + (r & (rows - 1))


def _diff_attn_kernel(lam_ref, sub_ref, q_ref, kv_ref, o_ref, *, lam_init, tk):
    i = pl.program_id(2)
    tq = q_ref.shape[0]
    q = q_ref[...]
    qa, qb = q[:, :LANES], q[:, LANES:]
    qs = jnp.concatenate([_low_half(qa), _low_half(qb), _high_half(qa), _high_half(qb)], axis=0).astype(BF16)
    qpos = _stacked_positions(i * tq, tq)

    def body(j, carry):
        start = pl.multiple_of(j * tk, tk)
        kvj = kv_ref[pl.ds(start, tk), :]
        kpos = j * tk + lax.broadcasted_iota(jnp.int32, (1, tk), 1)
        return _online_step(qs, kvj[:, :LANES], kvj[:, LANES:], kpos <= qpos, *carry)

    n_blocks = (i * tq + tq - 1) // tk + 1
    _, l, acc = lax.fori_loop(0, n_blocks, body, _online_init(4 * tq))
    o = acc / jnp.maximum(l, TINY)
    lam = _diff_lambda(lam_ref, lam_init)
    sub_g = sub_ref[...]
    o_ref[:, :LANES] = _diff_finish(o[0:tq], o[2 * tq:3 * tq], lam, sub_g, lam_init).astype(o_ref.dtype)
    o_ref[:, LANES:] = _diff_finish(o[tq:2 * tq], o[3 * tq:], lam, sub_g, lam_init).astype(o_ref.dtype)


def _diff_attn(q, kvb, lam_p, sub_g, lam_init, batch, tq, tk):
    m, nq = q.shape
    t = m // batch
    nt = t // tq
    gw = nq // DIFF_KV_HEADS
    return pl.pallas_call(
        functools.partial(_diff_attn_kernel, lam_init=lam_init, tk=tk),
        grid=(batch, DIFF_KV_HEADS, nt),
        in_specs=[
            pl.BlockSpec((4, HEAD_DIM), lambda b, g, i: (0, 0)),
            pl.BlockSpec((1, 2 * HEAD_DIM), lambda b, g, i: (0, 0)),
            pl.BlockSpec((tq, gw), lambda b, g, i: (b * nt + i, g)),
            pl.BlockSpec((t, gw), lambda b, g, i: (b, g)),
        ],
        out_specs=pl.BlockSpec((tq, gw), lambda b, g, i: (b * nt + i, g)),
        out_shape=jax.ShapeDtypeStruct((m, nq), BF16),
        compiler_params=_params("arbitrary", "arbitrary", "arbitrary"),
        name="diff_attn",
    )(lam_p, sub_g, q, kvb)


def _diff_sample_kernel(pt_ref, lam_ref, sub_ref, q_ref, kvn_ref, *refs, lam_init, n_pages):
    del pt_ref
    pages = refs[:n_pages]
    o_ref = refs[n_pages]
    lam = _diff_lambda(lam_ref, lam_init)
    sub_g = sub_ref[...]
    gw = 4 * HEAD_DIM
    for g in range(DIFF_KV_HEADS):
        qg = q_ref[:, g * gw:(g + 1) * gw]
        qa, qb = qg[:, :LANES], qg[:, LANES:]
        q4 = jnp.concatenate([_low_half(qa), _low_half(qb), _high_half(qa), _high_half(qb)], axis=0)
        q8 = jnp.concatenate([q4, q4], axis=0).astype(BF16)
        new = kvn_ref[:, g * gw:(g + 1) * gw].astype(BF16)
        pk = [pages[p][:, g, :].astype(BF16) for p in range(n_pages)]
        s = jnp.concatenate([_dot_nt(q8, kp[:, :LANES]) for kp in pk], axis=1)
        s_new = jnp.sum(q8.astype(F32) * new[:, :LANES].astype(F32), axis=-1, keepdims=True)
        mx = jnp.maximum(jnp.max(s, axis=-1, keepdims=True), s_new)
        p = jnp.exp(s - mx)
        p_new = jnp.exp(s_new - mx)
        den = jnp.sum(p, axis=-1, keepdims=True) + p_new
        pb = p.astype(BF16)
        psz = pk[0].shape[0]
        acc = p_new.astype(BF16).astype(F32) * new[:, LANES:].astype(F32)
        for pi, kp in enumerate(pk):
            acc = acc + _dot(pb[:, pi * psz:(pi + 1) * psz], kp[:, LANES:])
        o = acc / jnp.maximum(den, TINY)
        o_ref[:, g * gw:g * gw + LANES] = _diff_finish(o[0:1], o[2:3], lam, sub_g, lam_init).astype(o_ref.dtype)
        o_ref[:, g * gw + LANES:(g + 1) * gw] = _diff_finish(o[1:2], o[3:4], lam, sub_g, lam_init).astype(o_ref.dtype)


def _page_spec(shape_tail, layer, p):
    return pl.BlockSpec((None, None) + shape_tail, lambda b, pt: (layer, pt[b, p], 0, 0, 0))


def _diff_sample(q, kv_new, cache, layer, page_table, lam_p, sub_g, lam_init):
    m, nq = q.shape
    n_pages = page_table.shape[1]
    tail = cache.shape[2:]
    row = lambda b, pt: (b, 0, 0)
    grid_spec = pltpu.PrefetchScalarGridSpec(
        num_scalar_prefetch=1,
        grid=(m,),
        in_specs=[
            pl.BlockSpec((4, HEAD_DIM), lambda b, pt: (0, 0)),
            pl.BlockSpec((1, 2 * HEAD_DIM), lambda b, pt: (0, 0)),
            pl.BlockSpec((None, 1, nq), row),
            pl.BlockSpec((None, 1, kv_new.shape[1]), row),
        ] + [_page_spec(tail, layer, p) for p in range(n_pages)],
        out_specs=pl.BlockSpec((None, 1, nq), row),
    )
    out = pl.pallas_call(
        functools.partial(_diff_sample_kernel, lam_init=lam_init, n_pages=n_pages),
        grid_spec=grid_spec,
        out_shape=jax.ShapeDtypeStruct((m, 1, nq), BF16),
        compiler_params=_params("arbitrary"),
        name="diff_sample",
    )(page_table, lam_p, sub_g, q[:, None, :], kv_new[:, None, :], *([cache] * n_pages))
    return out[:, 0, :]


def _nsa_proj_kernel(x_ref, g_ref, w_ref, wc_ref, q_ref, rows_ref, rowsb_ref, win_ref, winb_ref,
                     gates_ref, ce_ref, co_ref):
    tm = x_ref.shape[0]
    h = _rms(x_ref[...], g_ref[...]).astype(BF16)
    p = _dot(h, w_ref[...])
    nq, nr, nw = q_ref.shape[1], rows_ref.shape[1], win_ref.shape[1]
    q_ref[...] = p[:, :nq] * QK_SCALE
    rows = p[:, nq:nq + nr]
    rows_ref[...] = rows
    rowsb_ref[...] = rows.astype(BF16)
    win = p[:, nq + nr:nq + nr + nw]
    win_ref[...] = win
    winb_ref[...] = win.astype(BF16)
    gl = p[:, nq + nr + nw:]
    gates_ref[...] = 1.0 / (1.0 + jnp.exp(-gl))
    if ce_ref is not None:
        wc = wc_ref[...]
        gw = nr // NSA_GROUPS
        for g in range(NSA_GROUPS):
            kc = rows[:, g * gw:g * gw + LANES].reshape(tm // SEL_BLOCK, SEL_BLOCK, LANES) * wc[None]
            ce_ref[:, g * LANES:(g + 1) * LANES] = jnp.sum(kc[:, :CMP_BLOCK], axis=1)
            co_ref[:, g * LANES:(g + 1) * LANES] = jnp.sum(kc[:, CMP_BLOCK:], axis=1)


def _nsa_proj_kernel_nocmp(x_ref, g_ref, w_ref, q_ref, rows_ref, rowsb_ref, win_ref, winb_ref, gates_ref):
    _nsa_proj_kernel(x_ref, g_ref, w_ref, None, q_ref, rows_ref, rowsb_ref, win_ref, winb_ref,
                     gates_ref, None, None)


_NSA_NQ = NSA_GROUPS * NSA_HPG * HEAD_DIM
_NSA_NR = NSA_GROUPS * 4 * HEAD_DIM
_NSA_NW = NSA_GROUPS * 2 * HEAD_DIM
_NSA_NG = NSA_GROUPS * LANES


def _nsa_proj(x, g, w, wc, tm, compress):
    m, d = x.shape
    n = w.shape[1]
    row = lambda i: (i, 0)
    in_specs = [
        pl.BlockSpec((tm, d), row),
        pl.BlockSpec((1, d), lambda i: (0, 0)),
        pl.BlockSpec((d, n), lambda i: (0, 0)),
    ]
    widths = [(_NSA_NQ, F32), (_NSA_NR, F32), (_NSA_NR, BF16), (_NSA_NW, F32), (_NSA_NW, BF16), (_NSA_NG, F32)]
    out_specs = [pl.BlockSpec((tm, wd), row) for wd, _ in widths]
    out_shape = [jax.ShapeDtypeStruct((m, wd), dt) for wd, dt in widths]
    args = [x, g, w]
    kern = _nsa_proj_kernel_nocmp
    if compress:
        nb = tm // SEL_BLOCK
        in_specs.append(pl.BlockSpec((SEL_BLOCK, LANES), lambda i: (0, 0)))
        args.append(wc)
        out_specs += [pl.BlockSpec((nb, NSA_GROUPS * LANES), row)] * 2
        out_shape += [jax.ShapeDtypeStruct((m // SEL_BLOCK, NSA_GROUPS * LANES), F32)] * 2
        kern = _nsa_proj_kernel
    return pl.pallas_call(
        kern,
        grid=(m // tm,),
        in_specs=in_specs,
        out_specs=out_specs,
        out_shape=out_shape,
        compiler_params=_params("arbitrary"),
        name="nsa_proj",
    )(*args)


def _nsa_queries(q):
    qa, qb = q[:, :LANES], q[:, LANES:]
    parts = [_low_half(qa), _low_half(_swap_halves(qa)), _low_half(qb), _low_half(_swap_halves(qb))]
    return jnp.concatenate(parts, axis=0).astype(BF16)


def _select_blocks(imp, n_keep):
    n = imp.shape[1]
    sidx = lax.broadcasted_iota(jnp.int32, imp.shape, 1)
    cnt = jnp.zeros(imp.shape, F32)
    for s2 in range(n):
        col = imp[:, s2:s2 + 1]
        beats = (col > imp) | ((col == imp) & (sidx > s2))
        cnt = cnt + jnp.where(beats, 1.0, 0.0)
    return jnp.where(cnt < n_keep, 1.0, 0.0)


def _nsa_gate_mix(gates, o_c, o_s, o_w, rows):
    mixed = []
    for h in range(NSA_HPG):
        sl = slice(h * rows, (h + 1) * rows)
        mixed.append(gates[:, 3 * h:3 * h + 1] * o_c[sl] + gates[:, 3 * h + 1:3 * h + 2] * o_s[sl]
                     + gates[:, 3 * h + 2:3 * h + 3] * o_w[sl])
    lane = lax.broadcasted_iota(jnp.int32, mixed[0].shape, 1)
    pair0 = jnp.where(lane < HEAD_DIM, _swap_halves(mixed[0]), mixed[1])
    pair1 = jnp.where(lane < HEAD_DIM, _swap_halves(mixed[2]), mixed[3])
    return pair0, pair1


def _nsa_attn_kernel(q_ref, gates_ref, ce_ref, co_ref, rows_ref, win_ref, emat_ref, o_ref, mask_ref, *, tk):
    i = pl.program_id(2)
    tq = q_ref.shape[0]
    t_all = rows_ref.shape[0]
    nkb = t_all // tk
    qs = _nsa_queries(q_ref[...])
    tpos = i * tq + lax.broadcasted_iota(jnp.int32, (tq, 1), 0)
    tpos4 = _stacked_positions(i * tq, tq)

    n_half = ce_ref.shape[0]
    kvc = jnp.concatenate([ce_ref[...], co_ref[...]], axis=0).astype(BF16)
    cidx = lax.broadcasted_iota(jnp.int32, (1, 2 * n_half), 1)
    cblk = jnp.where(cidx < n_half, 2 * cidx, 2 * (cidx - n_half) + 1)
    mask_c = ((cblk + 1) * CMP_BLOCK - 1) <= tpos4
    p_c = _masked_softmax(_dot_nt(qs, kvc), mask_c)
    o_c = _dot(p_c.astype(BF16), kvc)
    psum = p_c[0:tq] + p_c[tq:2 * tq] + p_c[2 * tq:3 * tq] + p_c[3 * tq:]
    imp = psum[:, :n_half] + psum[:, n_half:]
    sidx = lax.broadcasted_iota(jnp.int32, (1, n_half), 1)
    forced = (sidx == lax.shift_right_logical(tpos, int(math.log2(SEL_BLOCK)))) | (sidx == 0)
    valid = sidx * SEL_BLOCK <= tpos
    imp = jnp.where(forced, FORCE_SCORE, jnp.where(valid, imp, NEG))
    sel = _select_blocks(imp, min(N_SEL, n_half))

    selexp = _dot(sel.astype(BF16), emat_ref[...])
    for j in range(nkb):
        kpos = j * tk + lax.broadcasted_iota(jnp.int32, (1, tk), 1)
        mask_ref[j] = jnp.where(kpos <= tpos, selexp[:, j * tk:(j + 1) * tk], 0.0)

    def sel_body(j, carry):
        start = pl.multiple_of(j * tk, tk)
        kvj = rows_ref[pl.ds(start, tk), LANES:]
        return _online_step(qs, kvj, kvj, _tile4(mask_ref[j]) > 0.5, *carry)

    n_blocks = (i * tq + tq - 1) // tk + 1
    _, l_s, acc_s = lax.fori_loop(0, n_blocks, sel_body, _online_init(4 * tq))
    o_s = acc_s / jnp.maximum(l_s, TINY)

    carry = _online_init(4 * tq)
    first = i * (tq // tk) + (-(WINDOW - 1)) // tk
    n_win = (tq - 1) // tk - (-(WINDOW - 1)) // tk + 1
    for dj in range(n_win):
        j = first + dj
        jc = jnp.clip(j, 0, nkb - 1)
        start = pl.multiple_of(jc * tk, tk)
        kvj = win_ref[pl.ds(start, tk), :]
        kpos = j * tk + lax.broadcasted_iota(jnp.int32, (1, tk), 1)
        mask = (kpos <= tpos4) & (kpos > tpos4 - WINDOW) & (kpos >= 0)
        carry = _online_step(qs, kvj, kvj, mask, *carry)
    _, l_w, acc_w = carry
    o_w = acc_w / jnp.maximum(l_w, TINY)

    pair0, pair1 = _nsa_gate_mix(gates_ref[...], o_c, o_s, o_w, tq)
    o_ref[:, :LANES] = pair0.astype(o_ref.dtype)
    o_ref[:, LANES:] = pair1.astype(o_ref.dtype)


def _sel_expand_matrix(n_blocks, n_keys):
    blk = lax.broadcasted_iota(jnp.int32, (n_blocks, n_keys), 0)
    key = lax.broadcasted_iota(jnp.int32, (n_blocks, n_keys), 1)
    return jnp.where(key // SEL_BLOCK == blk, 1.0, 0.0).astype(BF16)


def _nsa_attn(q, gates, ce, co, rowsb, winb, batch, tq, tk):
    m, nq = q.shape
    t = m // batch
    nt = t // tq
    n_half = t // SEL_BLOCK
    gw = nq // NSA_GROUPS
    emat = _sel_expand_matrix(n_half, t)
    return pl.pallas_call(
        functools.partial(_nsa_attn_kernel, tk=tk),
        grid=(batch, NSA_GROUPS, nt),
        in_specs=[
            pl.BlockSpec((tq, gw), lambda b, g, i: (b * nt + i, g)),
            pl.BlockSpec((tq, LANES), lambda b, g, i: (b * nt + i, g)),
            pl.BlockSpec((n_half, LANES), lambda b, g, i: (b, g)),
            pl.BlockSpec((n_half, LANES), lambda b, g, i: (b, g)),
            pl.BlockSpec((t, gw), lambda b, g, i: (b, g)),
            pl.BlockSpec((t, LANES), lambda b, g, i: (b, g)),
            pl.BlockSpec((n_half, t), lambda b, g, i: (0, 0)),
        ],
        out_specs=pl.BlockSpec((tq, gw), lambda b, g, i: (b * nt + i, g)),
        out_shape=jax.ShapeDtypeStruct((m, nq), BF16),
        scratch_shapes=[pltpu.VMEM((t // tk, tq, tk), F32)],
        compiler_params=_params("arbitrary", "arbitrary", "arbitrary"),
        name="nsa_attn",
    )(q, gates, ce, co, rowsb, winb, emat)


def _nsa_sample_kernel(pt_ref, q_ref, gates_ref, rown_ref, winn_ref, wc_ref, emat_ref, wst_ref, *refs, n_pages):
    del pt_ref
    pages = refs[:n_pages]
    o_ref, wout_ref = refs[n_pages], refs[n_pages + 1]
    gw = 4 * HEAD_DIM
    psz = pages[0].shape[0]
    n_old = wst_ref.shape[0]
    wc = wc_ref[...]
    n_half = emat_ref.shape[0]
    for g in range(NSA_GROUPS):
        q4 = _nsa_queries(q_ref[:, g * gw:(g + 1) * gw])
        q8 = jnp.concatenate([q4, q4], axis=0)
        q8f = q8.astype(F32)
        pg = [pages[p][:, g, :] for p in range(n_pages)]
        new = rown_ref[:, g * gw:(g + 1) * gw].astype(BF16).astype(F32)

        ce, co = [], []
        for x in pg:
            kc = x[:, :LANES].reshape(psz // SEL_BLOCK, SEL_BLOCK, LANES) * wc[None]
            ce.append(jnp.sum(kc[:, :CMP_BLOCK], axis=1))
            co.append(jnp.sum(kc[:, CMP_BLOCK:], axis=1))
        kvc = jnp.concatenate(ce + co, axis=0).astype(BF16)
        s_c = _dot_nt(q8, kvc)
        p_c = _masked_softmax(s_c, jnp.full(s_c.shape, True))
        o_c = _dot(p_c.astype(BF16), kvc)
        psum = p_c[0:1] + p_c[1:2] + p_c[2:3] + p_c[3:4]
        imp = psum[:, :n_half] + psum[:, n_half:]
        sidx = lax.broadcasted_iota(jnp.int32, (1, n_half), 1)
        imp = jnp.where(sidx == 0, FORCE_SCORE, imp)
        sel = _select_blocks(jnp.concatenate([imp] * 8, axis=0), N_SEL - 1)

        selexp = _dot(sel.astype(BF16), emat_ref[...]) > 0.5
        pgb = [x[:, LANES:].astype(BF16) for x in pg]
        s_s = jnp.concatenate([_dot_nt(q8, x) for x in pgb], axis=1)
        s_s = jnp.where(selexp, s_s, NEG)
        s_new = jnp.sum(q8f * new[:, LANES:], axis=-1, keepdims=True)
        mx = jnp.maximum(jnp.max(s_s, axis=-1, keepdims=True), s_new)
        p_s = jnp.where(selexp, jnp.exp(s_s - mx), 0.0)
        p_new = jnp.exp(s_new - mx)
        den = jnp.sum(p_s, axis=-1, keepdims=True) + p_new
        pb = p_s.astype(BF16)
        acc = p_new.astype(BF16).astype(F32) * new[:, LANES:]
        for pi, x in enumerate(pgb):
            acc = acc + _dot(pb[:, pi * psz:(pi + 1) * psz], x)
        o_s = acc / jnp.maximum(den, TINY)

        wold = wst_ref[:, g, :].astype(BF16)
        wnew = winn_ref[:, g * LANES:(g + 1) * LANES].astype(BF16).astype(F32)
        s_w = _dot_nt(q8, wold)
        widx = lax.broadcasted_iota(jnp.int32, s_w.shape, 1)
        in_win = widx > n_old - WINDOW
        s_w = jnp.where(in_win, s_w, NEG)
        s_wn = jnp.sum(q8f * wnew, axis=-1, keepdims=True)
        mx = jnp.maximum(jnp.max(s_w, axis=-1, keepdims=True), s_wn)
        p_w = jnp.where(in_win, jnp.exp(s_w - mx), 0.0)
        p_wn = jnp.exp(s_wn - mx)
        den = jnp.sum(p_w, axis=-1, keepdims=True) + p_wn
        acc = p_wn.astype(BF16).astype(F32) * wnew + _dot(p_w.astype(BF16), wold)
        o_w = acc / jnp.maximum(den, TINY)

        gates = gates_ref[:, g * LANES:(g + 1) * LANES]
        pair0, pair1 = _nsa_gate_mix(gates, o_c[0:4], o_s[0:4], o_w[0:4], 1)
        o_ref[:, g * gw:g * gw + LANES] = pair0.astype(o_ref.dtype)
        o_ref[:, g * gw + LANES:(g + 1) * gw] = pair1.astype(o_ref.dtype)
        wout_ref[n_old - 1:n_old, g, :] = winn_ref[:, g * LANES:(g + 1) * LANES]
    wout_ref[0:n_old - 1] = wst_ref[1:n_old]


def _nsa_sample(q, gates, rows_new, win_new, wc, cache, win_state, layer, page_table):
    m, nq = q.shape
    n_pages = page_table.shape[1]
    psz = cache.shape[2]
    past = n_pages * psz
    tail = cache.shape[2:]
    wtail = win_state.shape[2:]
    emat = _sel_expand_matrix(past // SEL_BLOCK, past)
    row = lambda b, pt: (b, 0, 0)
    const = lambda b, pt: (0, 0)
    grid_spec = pltpu.PrefetchScalarGridSpec(
        num_scalar_prefetch=1,
        grid=(m,),
        in_specs=[
            pl.BlockSpec((None, 1, nq), row),
            pl.BlockSpec((None, 1, gates.shape[1]), row),
            pl.BlockSpec((None, 1, rows_new.shape[1]), row),
            pl.BlockSpec((None, 1, win_new.shape[1]), row),
            pl.BlockSpec(wc.shape, const),
            pl.BlockSpec(emat.shape, const),
            pl.BlockSpec((None, None) + wtail, lambda b, pt: (layer, b, 0, 0, 0)),
        ] + [_page_spec(tail, layer, p) for p in range(n_pages)],
        out_specs=[
            pl.BlockSpec((None, 1, nq), row),
            pl.BlockSpec((None,) + wtail, lambda b, pt: (b, 0, 0, 0)),
        ],
    )
    o, wout = pl.pallas_call(
        functools.partial(_nsa_sample_kernel, n_pages=n_pages),
        grid_spec=grid_spec,
        out_shape=[
            jax.ShapeDtypeStruct((m, 1, nq), BF16),
            jax.ShapeDtypeStruct((m,) + wtail, F32),
        ],
        compiler_params=_params("arbitrary"),
        name="nsa_sample",
    )(page_table, q[:, None, :], gates[:, None, :], rows_new[:, None, :], win_new[:, None, :], wc, emat,
      win_state, *([cache] * n_pages))
    return o[:, 0, :], wout


def _nsa_weight_layout(w_in):
    d = w_in.shape[0]
    kv0 = _NSA_NQ
    per_g = 6 * HEAD_DIM
    g0 = kv0 + NSA_GROUPS * per_g
    rows_idx = [kv0 + g * per_g + c for g in range(NSA_GROUPS) for c in range(4 * HEAD_DIM)]
    win_idx = [kv0 + g * per_g + 4 * HEAD_DIM + c for g in range(NSA_GROUPS) for c in range(2 * HEAD_DIM)]
    n_gate = NSA_HPG * 3
    gate_cols = []
    for g in range(NSA_GROUPS):
        gate_cols.append(w_in[:, g0 + g * n_gate:g0 + (g + 1) * n_gate])
        gate_cols.append(jnp.zeros((d, LANES - n_gate), w_in.dtype))
    return jnp.concatenate(
        [w_in[:, :kv0], w_in[:, jnp.array(rows_idx)], w_in[:, jnp.array(win_idx)]] + gate_cols, axis=1)


def kernel(x_prompt, x_sample, state_conv, cache_nsa, state_nsa_win, cache_diff, page_table, norm_g, ffn_w_in,
           ffn_w_out, conv_w_in, conv_w, conv_w_out, nsa_w_in, nsa_w_cmp, nsa_w_out, diff_w_in, diff_lambda,
           diff_subln_g, diff_w_out):
    batch, seq, d = x_prompt.shape
    dec_batch, dec_seq, _ = x_sample.shape
    assert dec_seq == 1
    depth = norm_g.shape[0]
    xp = x_prompt.reshape(batch * seq, d)
    xs = x_sample.reshape(dec_batch, d)
    tm = 512
    ffn_in = ffn_w_in.astype(BF16)
    ffn_out = ffn_w_out.astype(BF16)
    outs = {k: [] for k in ("conv_p", "conv_s", "nsa_p", "nsa_s", "win_p", "win_s", "diff_p", "diff_s")}
    for i in range(depth):
        g = [norm_g[i, k][None, :] for k in range(6)]
        kind, j = i % N_MIXERS, i // N_MIXERS
        xp = _ffn(xp, g[0], g[1], ffn_in[i, 0], ffn_out[i, 0], tm)
        xs = _ffn(xs, g[0], g[1], ffn_in[i, 0], ffn_out[i, 0], dec_batch)
        if kind == 0:
            w_in, w_out = conv_w_in[j].astype(BF16), conv_w_out[j].astype(BF16)
            xp, st = _conv_prompt(xp, g[2], g[3], w_in, conv_w[j], w_out, batch, tm)
            outs["conv_p"].append(st)
            xs, st = _conv_sample(xs, state_conv[j].reshape(dec_batch, -1), g[2], g[3], w_in, conv_w[j], w_out)
            outs["conv_s"].append(st.reshape(dec_batch, CONV_W - 1, d))
        elif kind == 1:
            w_in = _nsa_weight_layout(nsa_w_in[j]).astype(BF16)
            w_out = nsa_w_out[j].astype(BF16)
            wc = jnp.concatenate([nsa_w_cmp[j, 0], nsa_w_cmp[j, 1]], axis=1)
            wc = jnp.concatenate([wc, wc], axis=0)
            q, rows, rowsb, win, winb, gates, ce, co = _nsa_proj(xp, g[2], w_in, wc, tm, True)
            o = _nsa_attn(q, gates, ce, co, rowsb, winb, batch, 256, 256)
            xp = _post(xp, o, w_out, g[3], tm)
            outs["nsa_p"].append(rows.reshape(batch, seq, NSA_GROUPS, 4 * HEAD_DIM))
            wkeep = min(WINDOW, seq)
            outs["win_p"].append(win.reshape(batch, seq, NSA_GROUPS, 2 * HEAD_DIM)[:, seq - wkeep:])
            q, rows, _, win, _, gates = _nsa_proj(xs, g[2], w_in, None, dec_batch, False)
            o, wout = _nsa_sample(q, gates, rows, win, wc, cache_nsa, state_nsa_win, j, page_table)
            xs = _post(xs, o, w_out, g[3], dec_batch)
            outs["nsa_s"].append(rows.reshape(dec_batch, 1, NSA_GROUPS, 4 * HEAD_DIM))
            outs["win_s"].append(wout)
        else:
            lam_init = 0.8 - 0.6 * math.exp(-0.3 * i)
            w_in, w_out = diff_w_in[j].astype(BF16), diff_w_out[j].astype(BF16)
            sub_g = diff_subln_g[j][None, :]
            q, kv, kvb = _diff_proj(xp, g[2], w_in, tm)
            o = _diff_attn(q, kvb, diff_lambda[j], sub_g, lam_init, batch, 256, 256)
            xp = _post(xp, o, w_out, g[3], tm)
            outs["diff_p"].append(kv.reshape(batch, seq, DIFF_KV_HEADS, 4 * HEAD_DIM))
            q, kv, _ = _diff_proj(xs, g[2], w_in, dec_batch)
            o = _diff_sample(q, kv, cache_diff, j, page_table, diff_lambda[j], sub_g, lam_init)
            xs = _post(xs, o, w_out, g[3], dec_batch)
            outs["diff_s"].append(kv.reshape(dec_batch, 1, DIFF_KV_HEADS, 4 * HEAD_DIM))
        xp = _ffn(xp, g[4], g[5], ffn_in[i, 1], ffn_out[i, 1], tm)
        xs = _ffn(xs, g[4], g[5], ffn_in[i, 1], ffn_out[i, 1], dec_batch)
    st = lambda k: jnp.stack(outs[k])
    return (xp.reshape(batch, seq, d), xs.reshape(dec_batch, 1, d), st("conv_p"), st("conv_s"), st("nsa_p"),
            st("nsa_s"), st("win_p"), st("win_s"), st("diff_p"), st("diff_s"))
```

```python
import functools
import math

import jax
import jax.numpy as jnp
from jax import lax
from jax.experimental import pallas as pl
from jax.experimental.pallas import tpu as pltpu

F32 = jnp.float32
BF16 = jnp.bfloat16

EPS = 1e-6
NEG = -1e30
TINY = 1e-30
N_MIXERS = 3
CONV_W = 3
HEAD_DIM = 64
NSA_GROUPS = 4
NSA_HPG = 4
CMP_BLOCK = 32
SEL_BLOCK = 64
N_SEL = 8
WINDOW = 512
FORCE_SCORE = 1e4
DIFF_KV_HEADS = 4
DIFF_REP = 2
QK_SCALE = HEAD_DIM ** -0.5

LANES = 128
VMEM_LIMIT_BYTES = 56 * 1024 * 1024

_NT = (((1,), (1,)), ((), ()))


def _params(*sem):
    return pltpu.CompilerParams(dimension_semantics=sem, vmem_limit_bytes=VMEM_LIMIT_BYTES)


def _rms(x, g):
    return x * lax.rsqrt(jnp.mean(x * x, axis=-1, keepdims=True) + EPS) * g


def _dot(a, b):
    return jnp.dot(a, b, preferred_element_type=F32)


def _dot_nt(a, b):
    return lax.dot_general(a, b, _NT, preferred_element_type=F32)


def _masked_softmax(s, mask):
    s = jnp.where(mask, s, NEG)
    m = jnp.max(s, axis=-1, keepdims=True)
    p = jnp.where(mask, jnp.exp(s - m), 0.0)
    return p / jnp.maximum(jnp.sum(p, axis=-1, keepdims=True), TINY)


def _low_half(x):
    lane = lax.broadcasted_iota(jnp.int32, x.shape, 1)
    return jnp.where(lane < HEAD_DIM, x, 0.0)


def _high_half(x):
    lane = lax.broadcasted_iota(jnp.int32, x.shape, 1)
    return jnp.where(lane >= HEAD_DIM, x, 0.0)


def _swap_halves(x):
    return pltpu.roll(x, HEAD_DIM, 1)


def _ffn_kernel(x_ref, gpre_ref, gpost_ref, wi_ref, wo_ref, o_ref):
    f = wo_ref.shape[0]
    x = x_ref[...]
    xn = _rms(x, gpre_ref[...]).astype(BF16)
    gate = _dot(xn, wi_ref[:, :f])
    up = _dot(xn, wi_ref[:, f:])
    act = (gate * (1.0 / (1.0 + jnp.exp(-gate))) * up).astype(BF16)
    o_ref[...] = x + 0.5 * _rms(_dot(act, wo_ref[...]), gpost_ref[...])


def _resident(shape):
    return pl.BlockSpec(shape, lambda *_: (0,) * len(shape), pipeline_mode=pl.Buffered(1))


def _ffn(x, g_pre, g_post, w_in, w_out, tm):
    m, d = x.shape
    return pl.pallas_call(
        _ffn_kernel,
        grid=(m // tm,),
        in_specs=[
            pl.BlockSpec((tm, d), lambda i: (i, 0)),
            _resident((1, d)),
            _resident((1, d)),
            _resident(w_in.shape),
            _resident(w_out.shape),
        ],
        out_specs=pl.BlockSpec((tm, d), lambda i: (i, 0)),
        out_shape=jax.ShapeDtypeStruct((m, d), F32),
        compiler_params=_params("arbitrary"),
        name="half_ffn",
    )(x, g_pre, g_post, w_in, w_out)


def _post_kernel(x_ref, o_ref, w_ref, g_ref, y_ref):
    y = _dot(o_ref[...], w_ref[...])
    y_ref[...] = x_ref[...] + _rms(y, g_ref[...])


def _post(x, o, w, g, tm):
    m, d = x.shape
    k = o.shape[1]
    return pl.pallas_call(
        _post_kernel,
        grid=(m // tm,),
        in_specs=[
            pl.BlockSpec((tm, d), lambda i: (i, 0)),
            pl.BlockSpec((tm, k), lambda i: (i, 0)),
            pl.BlockSpec((k, d), lambda i: (0, 0)),
            pl.BlockSpec((1, d), lambda i: (0, 0)),
        ],
        out_specs=pl.BlockSpec((tm, d), lambda i: (i, 0)),
        out_shape=jax.ShapeDtypeStruct((m, d), F32),
        compiler_params=_params("arbitrary"),
        name="mixer_out",
    )(x, o, w, g)


def _conv_prompt_kernel(x_ref, g2_ref, g3_ref, win_ref, cw_ref, wout_ref, y_ref, st_ref, ubuf_ref):
    t = pl.program_id(1)
    tm, d = x_ref.shape
    x = x_ref[...]
    h = _rms(x, g2_ref[...]).astype(BF16)
    p = _dot(h, win_ref[...])
    c, b, v = p[:, :d], p[:, d:2 * d], p[:, 2 * d:]
    u = c * v

    @pl.when(t == 0)
    def _():
        ubuf_ref[0:8, :] = jnp.zeros((8, d), F32)

    ubuf_ref[8:8 + tm, :] = u
    cw = cw_ref[...]
    y = cw[0:1] * ubuf_ref[6:6 + tm, :] + cw[1:2] * ubuf_ref[7:7 + tm, :] + cw[2:3] * u
    out = _dot((b * y).astype(BF16), wout_ref[...])
    y_ref[...] = x + _rms(out, g3_ref[...])
    last2 = ubuf_ref[6 + tm:8 + tm, :]
    st_ref[...] = last2
    ubuf_ref[6:8, :] = last2


def _conv_prompt(x, g2, g3, w_in, cw, w_out, batch, tm):
    m, d = x.shape
    t = m // batch
    nt = t // tm
    return pl.pallas_call(
        _conv_prompt_kernel,
        grid=(batch, nt),
        in_specs=[
            pl.BlockSpec((tm, d), lambda b, i: (b * nt + i, 0)),
            pl.BlockSpec((1, d), lambda b, i: (0, 0)),
            pl.BlockSpec((1, d), lambda b, i: (0, 0)),
            pl.BlockSpec((d, 3 * d), lambda b, i: (0, 0)),
            pl.BlockSpec((CONV_W, d), lambda b, i: (0, 0)),
            pl.BlockSpec((d, d), lambda b, i: (0, 0)),
        ],
        out_specs=[
            pl.BlockSpec((tm, d), lambda b, i: (b * nt + i, 0)),
            pl.BlockSpec((None, CONV_W - 1, d), lambda b, i: (b, 0, 0)),
        ],
        out_shape=[
            jax.ShapeDtypeStruct((m, d), F32),
            jax.ShapeDtypeStruct((batch, CONV_W - 1, d), F32),
        ],
        scratch_shapes=[pltpu.VMEM((tm + 8, d), F32)],
        compiler_params=_params("arbitrary", "arbitrary"),
        name="conv_prompt",
    )(x, g2, g3, w_in, cw, w_out)


def _conv_sample_kernel(x_ref, st_ref, g2_ref, g3_ref, win_ref, cw_ref, wout_ref, y_ref, nst_ref):
    d = x_ref.shape[1]
    x = x_ref[...]
    h = _rms(x, g2_ref[...]).astype(BF16)
    p = _dot(h, win_ref[...])
    c, b, v = p[:, :d], p[:, d:2 * d], p[:, 2 * d:]
    u = c * v
    s0, s1 = st_ref[:, :d], st_ref[:, d:]
    cw = cw_ref[...]
    y = cw[0:1] * s0 + cw[1:2] * s1 + cw[2:3] * u
    out = _dot((b * y).astype(BF16), wout_ref[...])
    y_ref[...] = x + _rms(out, g3_ref[...])
    nst_ref[:, :d] = s1
    nst_ref[:, d:] = u


def _conv_sample(x, state, g2, g3, w_in, cw, w_out):
    m, d = x.shape
    return pl.pallas_call(
        _conv_sample_kernel,
        out_shape=[
            jax.ShapeDtypeStruct((m, d), F32),
            jax.ShapeDtypeStruct((m, (CONV_W - 1) * d), F32),
        ],
        compiler_params=pltpu.CompilerParams(vmem_limit_bytes=VMEM_LIMIT_BYTES),
        name="conv_sample",
    )(x, state, g2, g3, w_in, cw, w_out)


def _diff_proj_kernel(x_ref, g_ref, w_ref, q_ref, kv_ref, kvb_ref):
    h = _rms(x_ref[...], g_ref[...]).astype(BF16)
    p = _dot(h, w_ref[...])
    nq = q_ref.shape[1]
    q_ref[...] = p[:, :nq] * QK_SCALE
    kv = p[:, nq:]
    kv_ref[...] = kv
    kvb_ref[...] = kv.astype(BF16)


def _diff_proj(x, g, w, tm):
    m, d = x.shape
    n = w.shape[1]
    nq = DIFF_KV_HEADS * DIFF_REP * 2 * HEAD_DIM
    nkv = n - nq
    return pl.pallas_call(
        _diff_proj_kernel,
        grid=(m // tm,),
        in_specs=[
            pl.BlockSpec((tm, d), lambda i: (i, 0)),
            pl.BlockSpec((1, d), lambda i: (0, 0)),
            pl.BlockSpec((d, n), lambda i: (0, 0)),
        ],
        out_specs=[
            pl.BlockSpec((tm, nq), lambda i: (i, 0)),
            pl.BlockSpec((tm, nkv), lambda i: (i, 0)),
            pl.BlockSpec((tm, nkv), lambda i: (i, 0)),
        ],
        out_shape=[
            jax.ShapeDtypeStruct((m, nq), F32),
            jax.ShapeDtypeStruct((m, nkv), F32),
            jax.ShapeDtypeStruct((m, nkv), BF16),
        ],
        compiler_params=_params("arbitrary"),
        name="diff_proj",
    )(x, g, w)


def _diff_lambda(lam_ref, lam_init):
    lp = lam_ref[...]
    a = jnp.sum(lp[0:1] * lp[1:2], axis=-1, keepdims=True)
    b = jnp.sum(lp[2:3] * lp[3:4], axis=-1, keepdims=True)
    return jnp.exp(a) - jnp.exp(b) + lam_init


def _diff_finish(o1, o2, lam, sub_g, lam_init):
    dlt = o1 - lam * o2
    return _rms(dlt, sub_g) * (1.0 - lam_init)


def _online_step(q, kv_k, kv_v, bias, m_prev, l_prev, acc_prev):
    s = _dot_nt(q, kv_k)
    if bias is not None:
        rows, tk = s.shape
        s = (s.reshape(4, rows // 4, tk) + bias[None]).reshape(rows, tk)
    m_next = jnp.maximum(m_prev, jnp.max(s, axis=-1, keepdims=True))
    p = jnp.exp(s - m_next)
    alpha = jnp.exp(m_prev - m_next)
    l_next = alpha * l_prev + jnp.sum(p, axis=-1, keepdims=True)
    acc_next = alpha * acc_prev + _dot(p.astype(BF16), kv_v)
    return m_next, l_next, acc_next


def _online_init(rows):
    return (jnp.full((rows, 1), NEG, F32), jnp.zeros((rows, 1), F32), jnp.zeros((rows, LANES), F32))


def _attend_bias(ok):
    return jnp.where(ok, 0.0, NEG)


def _stacked_positions(first, rows):
    r = lax.broadcasted_iota(jnp.int32, (4 * rows, 1), 0)
    return first + (r & (rows - 1))


def _diff_attn_kernel(lam_ref, sub_ref, q_ref, kv_ref, o_ref, *, lam_init):
    i = pl.program_id(2)
    tq = q_ref.shape[0]
    q = q_ref[...]
    qa, qb = q[:, :LANES], q[:, LANES:]
    qs = jnp.concatenate([_low_half(qa), _low_half(qb), _high_half(qa), _high_half(qb)], axis=0).astype(BF16)

    def step(j, bias, carry):
        kvj = kv_ref[pl.ds(pl.multiple_of(j * tq, tq), tq), :]
        return _online_step(qs, kvj[:, :LANES], kvj[:, LANES:], bias, *carry)

    carry = lax.fori_loop(0, i, lambda j, c: step(j, None, c), _online_init(4 * tq))
    causal = lax.broadcasted_iota(jnp.int32, (tq, tq), 1) <= lax.broadcasted_iota(jnp.int32, (tq, tq), 0)
    _, l, acc = step(i, _attend_bias(causal), carry)
    o = acc / jnp.maximum(l, TINY)
    lam = _diff_lambda(lam_ref, lam_init)
    sub_g = sub_ref[...]
    o_ref[:, :LANES] = _diff_finish(o[0:tq], o[2 * tq:3 * tq], lam, sub_g, lam_init).astype(o_ref.dtype)
    o_ref[:, LANES:] = _diff_finish(o[tq:2 * tq], o[3 * tq:], lam, sub_g, lam_init).astype(o_ref.dtype)


def _diff_attn(q, kvb, lam_p, sub_g, lam_init, batch, tq):
    m, nq = q.shape
    t = m // batch
    nt = t // tq
    gw = nq // DIFF_KV_HEADS
    return pl.pallas_call(
        functools.partial(_diff_attn_kernel, lam_init=lam_init),
        grid=(batch, DIFF_KV_HEADS, nt),
        in_specs=[
            pl.BlockSpec((4, HEAD_DIM), lambda b, g, i: (0, 0)),
            pl.BlockSpec((1, 2 * HEAD_DIM), lambda b, g, i: (0, 0)),
            pl.BlockSpec((tq, gw), lambda b, g, i: (b * nt + i, g)),
            pl.BlockSpec((t, gw), lambda b, g, i: (b, g)),
        ],
        out_specs=pl.BlockSpec((tq, gw), lambda b, g, i: (b * nt + i, g)),
        out_shape=jax.ShapeDtypeStruct((m, nq), BF16),
        compiler_params=_params("arbitrary", "arbitrary", "arbitrary"),
        name="diff_attn",
    )(lam_p, sub_g, q, kvb)


GROUPS = 4
ROWS_PER_KEY = 8


def _interleaved_pages(cache):
    nl, n_p, s, g, w = cache.shape
    x = cache.reshape(nl, n_p, s, g, w // LANES, LANES).transpose(0, 1, 2, 4, 3, 5)
    return x.reshape(nl, n_p, s * (w // LANES) * g, LANES)


def _rows_per_group(x, width, offset):
    parts = []
    for g in range(GROUPS):
        piece = x[:, g * width + offset:g * width + offset + LANES]
        parts.append(jnp.concatenate([piece] * 4, axis=0))
    return jnp.concatenate(parts, axis=0)


def _diff_sample_kernel(pt_ref, lam_ref, sub_ref, q_ref, kvn_ref, *refs, lam_init, n_pages):
    del pt_ref
    pages = refs[:n_pages]
    o_ref = refs[n_pages]
    lam = _diff_lambda(lam_ref, lam_init)
    sub_g = sub_ref[...]
    gw = 4 * HEAD_DIM
    rows = []
    for g in range(GROUPS):
        qg = q_ref[:, g * gw:(g + 1) * gw]
        qa, qb = qg[:, :LANES], qg[:, LANES:]
        rows += [_low_half(qa), _low_half(qb), _high_half(qa), _high_half(qb)]
    q16 = jnp.concatenate(rows, axis=0).astype(BF16)
    new = kvn_ref[...].astype(BF16).astype(F32)
    k_new = _rows_per_group(new, gw, 0)
    v_new = _rows_per_group(new, gw, LANES)
    xs = [pages[p][...].astype(BF16) for p in range(n_pages)]
    psz = xs[0].shape[0]
    s = jnp.concatenate([_dot_nt(q16, x) for x in xs], axis=1)
    row_g = lax.shift_right_logical(lax.broadcasted_iota(jnp.int32, (4 * GROUPS, 1), 0), 2)
    col = lax.broadcasted_iota(jnp.int32, (1, s.shape[1]), 1)
    own = (col & (ROWS_PER_KEY - 1)) == row_g
    s = jnp.where(own, s, NEG)
    s_new = jnp.sum(q16.astype(F32) * k_new, axis=-1, keepdims=True)
    mx = jnp.maximum(jnp.max(s, axis=-1, keepdims=True), s_new)
    p = jnp.where(own, jnp.exp(s - mx), 0.0)
    p_new = jnp.exp(s_new - mx)
    den = jnp.sum(p, axis=-1, keepdims=True) + p_new
    acc = p_new.astype(BF16).astype(F32) * v_new
    for pi, x in enumerate(xs):
        pv = pltpu.roll(p[:, pi * psz:(pi + 1) * psz], GROUPS, 1).astype(BF16)
        acc = acc + _dot(pv, x)
    o = acc / jnp.maximum(den, TINY)
    for g in range(GROUPS):
        for r in range(DIFF_REP):
            a = 4 * g + r
            res = _diff_finish(o[a:a + 1], o[a + 2:a + 3], lam, sub_g, lam_init)
            o_ref[:, g * gw + r * LANES:g * gw + (r + 1) * LANES] = res.astype(o_ref.dtype)


def _page_spec(shape_tail, layer, p):
    return pl.BlockSpec((None, None) + shape_tail, lambda b, pt: (layer, pt[b, p], 0, 0))


def _diff_sample(q, kv_new, cache, layer, page_table, lam_p, sub_g, lam_init):
    m, nq = q.shape
    n_pages = page_table.shape[1]
    cache = _interleaved_pages(cache)
    tail = cache.shape[2:]
    row = lambda b, pt: (b, 0, 0)
    grid_spec = pltpu.PrefetchScalarGridSpec(
        num_scalar_prefetch=1,
        grid=(m,),
        in_specs=[
            pl.BlockSpec((4, HEAD_DIM), lambda b, pt: (0, 0)),
            pl.BlockSpec((1, 2 * HEAD_DIM), lambda b, pt: (0, 0)),
            pl.BlockSpec((None, 1, nq), row),
            pl.BlockSpec((None, 1, kv_new.shape[1]), row),
        ] + [_page_spec(tail, layer, p) for p in range(n_pages)],
        out_specs=pl.BlockSpec((None, 1, nq), row),
    )
    out = pl.pallas_call(
        functools.partial(_diff_sample_kernel, lam_init=lam_init, n_pages=n_pages),
        grid_spec=grid_spec,
        out_shape=jax.ShapeDtypeStruct((m, 1, nq), BF16),
        compiler_params=_params("arbitrary"),
        name="diff_sample",
    )(page_table, lam_p, sub_g, q[:, None, :], kv_new[:, None, :], *([cache] * n_pages))
    return out[:, 0, :]


def _nsa_proj_kernel(x_ref, g_ref, w_ref, wc_ref, q_ref, rows_ref, rowsb_ref, win_ref, winb_ref,
                     gates_ref, ce_ref, co_ref):
    tm = x_ref.shape[0]
    h = _rms(x_ref[...], g_ref[...]).astype(BF16)
    p = _dot(h, w_ref[...])
    nq, nr, nw = q_ref.shape[1], rows_ref.shape[1], win_ref.shape[1]
    q_ref[...] = p[:, :nq] * QK_SCALE
    rows = p[:, nq:nq + nr]
    rows_ref[...] = rows
    rowsb_ref[...] = rows.astype(BF16)
    win = p[:, nq + nr:nq + nr + nw]
    win_ref[...] = win
    winb_ref[...] = win.astype(BF16)
    gl = p[:, nq + nr + nw:]
    gates_ref[...] = 1.0 / (1.0 + jnp.exp(-gl))
    if ce_ref is not None:
        wc = wc_ref[...]
        gw = nr // NSA_GROUPS
        for g in range(NSA_GROUPS):
            kc = rows[:, g * gw:g * gw + LANES].reshape(tm // SEL_BLOCK, SEL_BLOCK, LANES) * wc[None]
            ce_ref[:, g * LANES:(g + 1) * LANES] = jnp.sum(kc[:, :CMP_BLOCK], axis=1)
            co_ref[:, g * LANES:(g + 1) * LANES] = jnp.sum(kc[:, CMP_BLOCK:], axis=1)


def _nsa_proj_kernel_nocmp(x_ref, g_ref, w_ref, q_ref, rows_ref, rowsb_ref, win_ref, winb_ref, gates_ref):
    _nsa_proj_kernel(x_ref, g_ref, w_ref, None, q_ref, rows_ref, rowsb_ref, win_ref, winb_ref,
                     gates_ref, None, None)


_NSA_NQ = NSA_GROUPS * NSA_HPG * HEAD_DIM
_NSA_NR = NSA_GROUPS * 4 * HEAD_DIM
_NSA_NW = NSA_GROUPS * 2 * HEAD_DIM
_NSA_NG = NSA_GROUPS * LANES


def _nsa_proj(x, g, w, wc, tm, compress):
    m, d = x.shape
    n = w.shape[1]
    row = lambda i: (i, 0)
    in_specs = [
        pl.BlockSpec((tm, d), row),
        pl.BlockSpec((1, d), lambda i: (0, 0)),
        pl.BlockSpec((d, n), lambda i: (0, 0)),
    ]
    widths = [(_NSA_NQ, F32), (_NSA_NR, F32), (_NSA_NR, BF16), (_NSA_NW, F32), (_NSA_NW, BF16), (_NSA_NG, F32)]
    out_specs = [pl.BlockSpec((tm, wd), row) for wd, _ in widths]
    out_shape = [jax.ShapeDtypeStruct((m, wd), dt) for wd, dt in widths]
    args = [x, g, w]
    kern = _nsa_proj_kernel_nocmp
    if compress:
        nb = tm // SEL_BLOCK
        in_specs.append(pl.BlockSpec((SEL_BLOCK, LANES), lambda i: (0, 0)))
        args.append(wc)
        out_specs += [pl.BlockSpec((nb, NSA_GROUPS * LANES), row)] * 2
        out_shape += [jax.ShapeDtypeStruct((m // SEL_BLOCK, NSA_GROUPS * LANES), F32)] * 2
        kern = _nsa_proj_kernel
    return pl.pallas_call(
        kern,
        grid=(m // tm,),
        in_specs=in_specs,
        out_specs=out_specs,
        out_shape=out_shape,
        compiler_params=_params("arbitrary"),
        name="nsa_proj",
    )(*args)


def _nsa_queries(q):
    qa, qb = q[:, :LANES], q[:, LANES:]
    parts = [_low_half(qa), _low_half(_swap_halves(qa)), _low_half(qb), _low_half(_swap_halves(qb))]
    return jnp.concatenate(parts, axis=0).astype(BF16)


def _select_blocks(imp, n_keep):
    n = imp.shape[1]
    sidx = lax.broadcasted_iota(jnp.int32, imp.shape, 1)
    cnt = jnp.zeros(imp.shape, F32)
    for s2 in range(n):
        col = imp[:, s2:s2 + 1]
        beats = (col > imp) | ((col == imp) & (sidx > s2))
        cnt = cnt + jnp.where(beats, 1.0, 0.0)
    return jnp.where(cnt < n_keep, 1.0, 0.0)


def _nsa_gate_mix(gates, o_c, o_s, o_w, rows):
    mixed = []
    for h in range(NSA_HPG):
        sl = slice(h * rows, (h + 1) * rows)
        mixed.append(gates[:, 3 * h:3 * h + 1] * o_c[sl] + gates[:, 3 * h + 1:3 * h + 2] * o_s[sl]
                     + gates[:, 3 * h + 2:3 * h + 3] * o_w[sl])
    lane = lax.broadcasted_iota(jnp.int32, mixed[0].shape, 1)
    pair0 = jnp.where(lane < HEAD_DIM, _swap_halves(mixed[0]), mixed[1])
    pair1 = jnp.where(lane < HEAD_DIM, _swap_halves(mixed[2]), mixed[3])
    return pair0, pair1


def _select_blocks_t(imp_t, n_keep):
    n = imp_t.shape[0]
    sidx = lax.broadcasted_iota(jnp.int32, imp_t.shape, 0)
    cnt = jnp.zeros(imp_t.shape, F32)
    for s2 in range(n):
        row = imp_t[s2:s2 + 1, :]
        beats = (row > imp_t) | ((row == imp_t) & (sidx > s2))
        cnt = cnt + jnp.where(beats, 1.0, 0.0)
    return jnp.where(cnt < n_keep, 1.0, 0.0)


def _nsa_attn_kernel(q_ref, gates_ref, ce_ref, co_ref, rows_ref, win_ref, emat_ref, o_ref):
    i = pl.program_id(2)
    tq = q_ref.shape[0]
    qs = _nsa_queries(q_ref[...])
    tpos = i * tq + lax.broadcasted_iota(jnp.int32, (tq, 1), 0)
    tpos4 = _stacked_positions(i * tq, tq)
    kpos0 = lax.broadcasted_iota(jnp.int32, (1, tq), 1)

    n_half = ce_ref.shape[0]
    kvc = jnp.concatenate([ce_ref[...], co_ref[...]], axis=0).astype(BF16)
    cidx = lax.broadcasted_iota(jnp.int32, (1, 2 * n_half), 1)
    cblk = jnp.where(cidx < n_half, 2 * cidx, 2 * (cidx - n_half) + 1)
    mask_c = ((cblk + 1) * CMP_BLOCK - 1) <= tpos4
    p_c = _masked_softmax(_dot_nt(qs, kvc), mask_c)
    o_c = _dot(p_c.astype(BF16), kvc)

    tpos_t = i * tq + kpos0
    tpos4_t = jnp.concatenate([tpos_t] * 4, axis=1)
    ridx = lax.broadcasted_iota(jnp.int32, (2 * n_half, 1), 0)
    rblk = jnp.where(ridx < n_half, 2 * ridx, 2 * (ridx - n_half) + 1)
    mask_t = ((rblk + 1) * CMP_BLOCK - 1) <= tpos4_t
    s_t = jnp.where(mask_t, _dot_nt(kvc, qs), NEG)
    p_t = jnp.where(mask_t, jnp.exp(s_t - jnp.max(s_t, axis=0, keepdims=True)), 0.0)
    p_t = p_t / jnp.maximum(jnp.sum(p_t, axis=0, keepdims=True), TINY)
    psum = p_t[:, 0:tq] + p_t[:, tq:2 * tq] + p_t[:, 2 * tq:3 * tq] + p_t[:, 3 * tq:]
    imp = psum[:n_half] + psum[n_half:]
    sidx = lax.broadcasted_iota(jnp.int32, (n_half, 1), 0)
    forced = (sidx == lax.shift_right_logical(tpos_t, int(math.log2(SEL_BLOCK)))) | (sidx == 0)
    valid = sidx * SEL_BLOCK <= tpos_t
    imp = jnp.where(forced, FORCE_SCORE, jnp.where(valid, imp, NEG))
    sel_t = _select_blocks_t(imp, min(N_SEL, n_half)).astype(BF16)
    eye = jnp.where(lax.broadcasted_iota(jnp.int32, (tq, tq), 0) == kpos0, 1.0, 0.0).astype(BF16)
    sel = _dot_nt(eye, sel_t).astype(BF16)

    def sel_step(j, extra_ok, carry):
        kvj = rows_ref[pl.ds(pl.multiple_of(j * tq, tq), tq), LANES:]
        ok = _dot(sel, emat_ref[j]) > 0.5
        if extra_ok is not None:
            ok = ok & extra_ok
        return _online_step(qs, kvj, kvj, _attend_bias(ok), *carry)

    carry = lax.fori_loop(0, i, lambda j, c: sel_step(j, None, c), _online_init(4 * tq))
    causal = kpos0 <= lax.broadcasted_iota(jnp.int32, (tq, 1), 0)
    _, l_s, acc_s = sel_step(i, causal, carry)
    o_s = acc_s / jnp.maximum(l_s, TINY)

    n_back = WINDOW // tq
    carry = _online_init(4 * tq)
    for dj in range(n_back, -1, -1):
        j = i - dj
        kvj = win_ref[pl.ds(pl.multiple_of(jnp.maximum(j, 0) * tq, tq), tq), :]
        kpos = j * tq + kpos0
        if dj == n_back:
            ok = (kpos > tpos - WINDOW) & (kpos >= 0)
        elif dj == 0:
            ok = causal
        else:
            ok = kpos >= 0
        carry = _online_step(qs, kvj, kvj, _attend_bias(ok), *carry)
    _, l_w, acc_w = carry
    o_w = acc_w / jnp.maximum(l_w, TINY)

    pair0, pair1 = _nsa_gate_mix(gates_ref[...], o_c, o_s, o_w, tq)
    o_ref[:, :LANES] = pair0.astype(o_ref.dtype)
    o_ref[:, LANES:] = pair1.astype(o_ref.dtype)


def _sel_expand_matrix(n_blocks, n_keys, tk):
    blk = lax.broadcasted_iota(jnp.int32, (n_blocks, n_keys), 0)
    key = lax.broadcasted_iota(jnp.int32, (n_blocks, n_keys), 1)
    e = jnp.where(key // SEL_BLOCK == blk, 1.0, 0.0).astype(BF16)
    return e.reshape(n_blocks, n_keys // tk, tk).transpose(1, 0, 2)


def _nsa_attn(q, gates, ce, co, rowsb, winb, batch, tq):
    m, nq = q.shape
    t = m // batch
    nt = t // tq
    n_half = t // SEL_BLOCK
    gw = nq // NSA_GROUPS
    assert WINDOW % tq == 0
    emat = _sel_expand_matrix(n_half, t, tq)
    return pl.pallas_call(
        _nsa_attn_kernel,
        grid=(batch, NSA_GROUPS, nt),
        in_specs=[
            pl.BlockSpec((tq, gw), lambda b, g, i: (b * nt + i, g)),
            pl.BlockSpec((tq, LANES), lambda b, g, i: (b * nt + i, g)),
            pl.BlockSpec((n_half, LANES), lambda b, g, i: (b, g)),
            pl.BlockSpec((n_half, LANES), lambda b, g, i: (b, g)),
            pl.BlockSpec((t, gw), lambda b, g, i: (b, g)),
            pl.BlockSpec((t, LANES), lambda b, g, i: (b, g)),
            pl.BlockSpec((nt, n_half, tq), lambda b, g, i: (0, 0, 0)),
        ],
        out_specs=pl.BlockSpec((tq, gw), lambda b, g, i: (b * nt + i, g)),
        out_shape=jax.ShapeDtypeStruct((m, nq), BF16),
        compiler_params=_params("arbitrary", "arbitrary", "arbitrary"),
        name="nsa_attn",
    )(q, gates, ce, co, rowsb, winb, emat)


def _softmax_with_new(s, ok, s_new):
    s = jnp.where(ok, s, NEG)
    mx = jnp.maximum(jnp.max(s, axis=-1, keepdims=True), s_new)
    p = jnp.where(ok, jnp.exp(s - mx), 0.0)
    p_new = jnp.exp(s_new - mx)
    return p, p_new, jnp.sum(p, axis=-1, keepdims=True) + p_new


def _nsa_sample_kernel(pt_ref, q_ref, gates_ref, rown_ref, winn_ref, wrep_ref, wst_ref, *refs, n_pages):
    del pt_ref
    pages = refs[:n_pages]
    o_ref, wout_ref, cmp_ref = refs[n_pages:n_pages + 3]
    gw = 4 * HEAD_DIM
    psz = pages[0].shape[0]
    n_wrows = wst_ref.shape[0]
    n_old = n_wrows // GROUPS
    n_half = cmp_ref.shape[0]
    q16 = jnp.concatenate([_nsa_queries(q_ref[:, g * gw:(g + 1) * gw]) for g in range(GROUPS)], axis=0)
    q16f = q16.astype(F32)
    row_g = lax.shift_right_logical(lax.broadcasted_iota(jnp.int32, (4 * GROUPS, 1), 0), 2)
    xs = [pages[p][...] for p in range(n_pages)]

    wrep = wrep_ref[...]
    nb = psz // ROWS_PER_KEY // CMP_BLOCK
    for p, x in enumerate(xs):
        blocks = jnp.sum((x * wrep).reshape(nb, CMP_BLOCK, ROWS_PER_KEY, LANES), axis=1)
        cmp_ref[p * nb // 2:(p + 1) * nb // 2] = blocks.reshape(nb // 2, 2, ROWS_PER_KEY, LANES)

    o_c, sel16 = [], []
    for g in range(GROUPS):
        q8 = jnp.concatenate([q16[4 * g:4 * g + 4]] * 2, axis=0)
        kvc = jnp.concatenate([cmp_ref[:, 0, g, :], cmp_ref[:, 1, g, :]], axis=0).astype(BF16)
        s_c = _dot_nt(q8, kvc)
        p_c = _masked_softmax(s_c, jnp.full(s_c.shape, True))
        o_c.append(_dot(p_c.astype(BF16), kvc)[0:4])
        psum = p_c[0:1] + p_c[1:2] + p_c[2:3] + p_c[3:4]
        imp = psum[:, :n_half] + psum[:, n_half:]
        sidx = lax.broadcasted_iota(jnp.int32, (1, n_half), 1)
        imp = jnp.where(sidx == 0, FORCE_SCORE, imp)
        sel16.append(_select_blocks(jnp.concatenate([imp] * 4, axis=0), N_SEL - 1))
    o_c = jnp.concatenate(o_c, axis=0)
    sel16 = jnp.concatenate(sel16, axis=0)

    xb = [x.astype(BF16) for x in xs]
    s_s = jnp.concatenate([_dot_nt(q16, x) for x in xb], axis=1)
    col = lax.broadcasted_iota(jnp.int32, (1, psz), 1)
    own = (col & (ROWS_PER_KEY - 1)) == (row_g + GROUPS)
    first_half = col < psz // 2
    ok = []
    for p in range(n_pages):
        chosen = jnp.where(first_half, sel16[:, 2 * p:2 * p + 1], sel16[:, 2 * p + 1:2 * p + 2]) > 0.5
        ok.append(own & chosen)
    new_rows = rown_ref[...].astype(BF16).astype(F32)
    ks_new = _rows_per_group(new_rows, gw, LANES)
    s_new = jnp.sum(q16f * ks_new, axis=-1, keepdims=True)
    p_s, p_new, den = _softmax_with_new(s_s, jnp.concatenate(ok, axis=1), s_new)
    acc = p_new.astype(BF16).astype(F32) * ks_new
    for pi, x in enumerate(xb):
        acc = acc + _dot(p_s[:, pi * psz:(pi + 1) * psz].astype(BF16), x)
    o_s = acc / jnp.maximum(den, TINY)

    wb = wst_ref[...].astype(BF16)
    wcol = lax.broadcasted_iota(jnp.int32, (1, n_wrows), 1)
    ok_w = ((wcol & (GROUPS - 1)) == row_g) & (lax.shift_right_logical(wcol, 2) > n_old - WINDOW)
    w_new = _rows_per_group(winn_ref[...].astype(BF16).astype(F32), LANES, 0)
    s_wn = jnp.sum(q16f * w_new, axis=-1, keepdims=True)
    p_w, p_wn, den = _softmax_with_new(_dot_nt(q16, wb), ok_w, s_wn)
    acc = p_wn.astype(BF16).astype(F32) * w_new + _dot(p_w.astype(BF16), wb)
    o_w = acc / jnp.maximum(den, TINY)

    for g in range(GROUPS):
        sl = slice(4 * g, 4 * g + 4)
        pair0, pair1 = _nsa_gate_mix(gates_ref[:, g * LANES:(g + 1) * LANES], o_c[sl], o_s[sl], o_w[sl], 1)
        o_ref[:, g * gw:g * gw + LANES] = pair0.astype(o_ref.dtype)
        o_ref[:, g * gw + LANES:(g + 1) * gw] = pair1.astype(o_ref.dtype)

    wout_ref[0:n_wrows - GROUPS, :] = wst_ref[GROUPS:n_wrows, :]
    wout_ref[n_wrows - GROUPS:n_wrows, :] = jnp.concatenate(
        [winn_ref[:, g * LANES:(g + 1) * LANES] for g in range(GROUPS)], axis=0)


def _nsa_sample(q, gates, rows_new, win_new, wc32, cache, win_state, layer, page_table):
    m, nq = q.shape
    n_pages = page_table.shape[1]
    cache = _interleaved_pages(cache)
    tail = cache.shape[2:]
    keys_per_page = tail[0] // ROWS_PER_KEY
    n_cmp = n_pages * keys_per_page // CMP_BLOCK
    nl, _, n_old, wg, ww = win_state.shape
    win_rows = win_state.reshape(nl, m, n_old * wg, ww)
    wrep = jnp.repeat(jnp.tile(wc32, (keys_per_page // CMP_BLOCK, 1)), ROWS_PER_KEY, axis=0)
    row = lambda b, pt: (b, 0, 0)
    grid_spec = pltpu.PrefetchScalarGridSpec(
        num_scalar_prefetch=1,
        grid=(m,),
        in_specs=[
            pl.BlockSpec((None, 1, nq), row),
            pl.BlockSpec((None, 1, gates.shape[1]), row),
            pl.BlockSpec((None, 1, rows_new.shape[1]), row),
            pl.BlockSpec((None, 1, win_new.shape[1]), row),
            pl.BlockSpec(wrep.shape, lambda b, pt: (0, 0)),
            pl.BlockSpec((None, None, n_old * wg, ww), lambda b, pt: (layer, b, 0, 0)),
        ] + [_page_spec(tail, layer, p) for p in range(n_pages)],
        out_specs=[
            pl.BlockSpec((None, 1, nq), row),
            pl.BlockSpec((None, n_old * wg, ww), row),
        ],
        scratch_shapes=[pltpu.VMEM((n_cmp // 2, 2, ROWS_PER_KEY, LANES), F32)],
    )
    o, wout = pl.pallas_call(
        functools.partial(_nsa_sample_kernel, n_pages=n_pages),
        grid_spec=grid_spec,
        out_shape=[
            jax.ShapeDtypeStruct((m, 1, nq), BF16),
            jax.ShapeDtypeStruct((m, n_old * wg, ww), F32),
        ],
        compiler_params=_params("arbitrary"),
        name="nsa_sample",
    )(page_table, q[:, None, :], gates[:, None, :], rows_new[:, None, :], win_new[:, None, :], wrep,
      win_rows, *([cache] * n_pages))
    return o[:, 0, :], wout.reshape(m, n_old, wg, ww)


def _nsa_weight_layout(w_in):
    d = w_in.shape[0]
    kv0 = _NSA_NQ
    per_g = 6 * HEAD_DIM
    g0 = kv0 + NSA_GROUPS * per_g
    rows_idx = [kv0 + g * per_g + c for g in range(NSA_GROUPS) for c in range(4 * HEAD_DIM)]
    win_idx = [kv0 + g * per_g + 4 * HEAD_DIM + c for g in range(NSA_GROUPS) for c in range(2 * HEAD_DIM)]
    n_gate = NSA_HPG * 3
    gate_cols = []
    for g in range(NSA_GROUPS):
        gate_cols.append(w_in[:, g0 + g * n_gate:g0 + (g + 1) * n_gate])
        gate_cols.append(jnp.zeros((d, LANES - n_gate), w_in.dtype))
    return jnp.concatenate(
        [w_in[:, :kv0], w_in[:, jnp.array(rows_idx)], w_in[:, jnp.array(win_idx)]] + gate_cols, axis=1)


def kernel(x_prompt, x_sample, state_conv, cache_nsa, state_nsa_win, cache_diff, page_table, norm_g, ffn_w_in,
           ffn_w_out, conv_w_in, conv_w, conv_w_out, nsa_w_in, nsa_w_cmp, nsa_w_out, diff_w_in, diff_lambda,
           diff_subln_g, diff_w_out):
    batch, seq, d = x_prompt.shape
    dec_batch, dec_seq, _ = x_sample.shape
    assert dec_seq == 1
    depth = norm_g.shape[0]
    xp = x_prompt.reshape(batch * seq, d)
    xs = x_sample.reshape(dec_batch, d)
    tm = 512
    ffn_in = ffn_w_in.astype(BF16)
    ffn_out = ffn_w_out.astype(BF16)
    outs = {k: [] for k in ("conv_p", "conv_s", "nsa_p", "nsa_s", "win_p", "win_s", "diff_p", "diff_s")}
    for i in range(depth):
        g = [norm_g[i, k][None, :] for k in range(6)]
        kind, j = i % N_MIXERS, i // N_MIXERS
        xp = _ffn(xp, g[0], g[1], ffn_in[i, 0], ffn_out[i, 0], tm)
        xs = _ffn(xs, g[0], g[1], ffn_in[i, 0], ffn_out[i, 0], dec_batch)
        if kind == 0:
            w_in, w_out = conv_w_in[j].astype(BF16), conv_w_out[j].astype(BF16)
            xp, st = _conv_prompt(xp, g[2], g[3], w_in, conv_w[j], w_out, batch, tm)
            outs["conv_p"].append(st)
            xs, st = _conv_sample(xs, state_conv[j].reshape(dec_batch, -1), g[2], g[3], w_in, conv_w[j], w_out)
            outs["conv_s"].append(st.reshape(dec_batch, CONV_W - 1, d))
        elif kind == 1:
            w_in = _nsa_weight_layout(nsa_w_in[j]).astype(BF16)
            w_out = nsa_w_out[j].astype(BF16)
            wc32 = jnp.concatenate([nsa_w_cmp[j, 0], nsa_w_cmp[j, 1]], axis=1)
            wc = jnp.concatenate([wc32, wc32], axis=0)
            q, rows, rowsb, win, winb, gates, ce, co = _nsa_proj(xp, g[2], w_in, wc, tm, True)
            o = _nsa_attn(q, gates, ce, co, rowsb, winb, batch, 256)
            xp = _post(xp, o, w_out, g[3], tm)
            outs["nsa_p"].append(rows.reshape(batch, seq, NSA_GROUPS, 4 * HEAD_DIM))
            wkeep = min(WINDOW, seq)
            outs["win_p"].append(win.reshape(batch, seq, NSA_GROUPS, 2 * HEAD_DIM)[:, seq - wkeep:])
            q, rows, _, win, _, gates = _nsa_proj(xs, g[2], w_in, None, dec_batch, False)
            o, wout = _nsa_sample(q, gates, rows, win, wc32, cache_nsa, state_nsa_win, j, page_table)
            xs = _post(xs, o, w_out, g[3], dec_batch)
            outs["nsa_s"].append(rows.reshape(dec_batch, 1, NSA_GROUPS, 4 * HEAD_DIM))
            outs["win_s"].append(wout)
        else:
            lam_init = 0.8 - 0.6 * math.exp(-0.3 * i)
            w_in, w_out = diff_w_in[j].astype(BF16), diff_w_out[j].astype(BF16)
            sub_g = diff_subln_g[j][None, :]
            q, kv, kvb = _diff_proj(xp, g[2], w_in, tm)
            o = _diff_attn(q, kvb, diff_lambda[j], sub_g, lam_init, batch, 256)
            xp = _post(xp, o, w_out, g[3], tm)
            outs["diff_p"].append(kv.reshape(batch, seq, DIFF_KV_HEADS, 4 * HEAD_DIM))
            q, kv, _ = _diff_proj(xs, g[2], w_in, dec_batch)
            o = _diff_sample(q, kv, cache_diff, j, page_table, diff_lambda[j], sub_g, lam_init)
            xs = _post(xs, o, w_out, g[3], dec_batch)
            outs["diff_s"].append(kv.reshape(dec_batch, 1, DIFF_KV_HEADS, 4 * HEAD_DIM))
        xp = _ffn(xp, g[4], g[5], ffn_in[i, 1], ffn_out[i, 1], tm)
        xs = _ffn(xs, g[4], g[5], ffn_in[i, 1], ffn_out[i, 1], dec_batch)
    st = lambda k: jnp.stack(outs[k])
    return (xp.reshape(batch, seq, d), xs.reshape(dec_batch, 1, d), st("conv_p"), st("conv_s"), st("nsa_p"),
            st("nsa_s"), st("win_p"), st("win_s"), st("diff_p"), st("diff_s"))
```

```python
import functools
import math

import jax
import jax.numpy as jnp
from jax import lax
from jax.experimental import pallas as pl
from jax.experimental.pallas import tpu as pltpu

F32 = jnp.float32
BF16 = jnp.bfloat16

EPS = 1e-6
NEG = -1e30
TINY = 1e-30
N_MIXERS = 3
CONV_W = 3
HEAD_DIM = 64
NSA_GROUPS = 4
NSA_HPG = 4
CMP_BLOCK = 32
SEL_BLOCK = 64
N_SEL = 8
WINDOW = 512
FORCE_SCORE = 1e4
DIFF_KV_HEADS = 4
DIFF_REP = 2
QK_SCALE = HEAD_DIM ** -0.5

LANES = 128
VMEM_LIMIT_BYTES = 56 * 1024 * 1024

_NT = (((1,), (1,)), ((), ()))


def _params(*sem):
    return pltpu.CompilerParams(dimension_semantics=sem, vmem_limit_bytes=VMEM_LIMIT_BYTES)


def _rms(x, g):
    return x * lax.rsqrt(jnp.mean(x * x, axis=-1, keepdims=True) + EPS) * g


def _dot(a, b):
    return jnp.dot(a, b, preferred_element_type=F32)


def _dot_nt(a, b):
    return lax.dot_general(a, b, _NT, preferred_element_type=F32)


def _masked_softmax(s, mask):
    s = jnp.where(mask, s, NEG)
    m = jnp.max(s, axis=-1, keepdims=True)
    p = jnp.where(mask, jnp.exp(s - m), 0.0)
    return p / jnp.maximum(jnp.sum(p, axis=-1, keepdims=True), TINY)


def _low_half(x):
    lane = lax.broadcasted_iota(jnp.int32, x.shape, 1)
    return jnp.where(lane < HEAD_DIM, x, 0.0)


def _high_half(x):
    lane = lax.broadcasted_iota(jnp.int32, x.shape, 1)
    return jnp.where(lane >= HEAD_DIM, x, 0.0)


def _swap_halves(x):
    return pltpu.roll(x, HEAD_DIM, 1)


def _ffn_kernel(x_ref, gpre_ref, gpost_ref, wi_ref, wo_ref, o_ref):
    f = wo_ref.shape[0]
    x = x_ref[...]
    xn = _rms(x, gpre_ref[...]).astype(BF16)
    gate = _dot(xn, wi_ref[:, :f])
    up = _dot(xn, wi_ref[:, f:])
    act = (gate * (1.0 / (1.0 + jnp.exp(-gate))) * up).astype(BF16)
    o_ref[...] = x + 0.5 * _rms(_dot(act, wo_ref[...]), gpost_ref[...])


def _resident(shape):
    return pl.BlockSpec(shape, lambda *_: (0,) * len(shape), pipeline_mode=pl.Buffered(1))


def _ffn(x, g_pre, g_post, w_in, w_out, layer, half, tm):
    m, d = x.shape

    def picked(w):
        return pl.BlockSpec((None, None) + w.shape[2:], lambda i: (layer, half, 0, 0),
                            pipeline_mode=pl.Buffered(1))

    return pl.pallas_call(
        _ffn_kernel,
        grid=(m // tm,),
        in_specs=[
            pl.BlockSpec((tm, d), lambda i: (i, 0)),
            _resident((1, d)),
            _resident((1, d)),
            picked(w_in),
            picked(w_out),
        ],
        out_specs=pl.BlockSpec((tm, d), lambda i: (i, 0)),
        out_shape=jax.ShapeDtypeStruct((m, d), F32),
        compiler_params=_params("arbitrary"),
        name="half_ffn",
    )(x, g_pre, g_post, w_in, w_out)


def _post_kernel(x_ref, o_ref, w_ref, g_ref, y_ref):
    y = _dot(o_ref[...], w_ref[...])
    y_ref[...] = x_ref[...] + _rms(y, g_ref[...])


def _post(x, o, w, g, tm):
    m, d = x.shape
    k = o.shape[1]
    return pl.pallas_call(
        _post_kernel,
        grid=(m // tm,),
        in_specs=[
            pl.BlockSpec((tm, d), lambda i: (i, 0)),
            pl.BlockSpec((tm, k), lambda i: (i, 0)),
            pl.BlockSpec((k, d), lambda i: (0, 0)),
            pl.BlockSpec((1, d), lambda i: (0, 0)),
        ],
        out_specs=pl.BlockSpec((tm, d), lambda i: (i, 0)),
        out_shape=jax.ShapeDtypeStruct((m, d), F32),
        compiler_params=_params("arbitrary"),
        name="mixer_out",
    )(x, o, w, g)


def _conv_prompt_kernel(x_ref, g2_ref, g3_ref, win_ref, cw_ref, wout_ref, y_ref, st_ref, ubuf_ref):
    t = pl.program_id(1)
    tm, d = x_ref.shape
    x = x_ref[...]
    h = _rms(x, g2_ref[...]).astype(BF16)
    p = _dot(h, win_ref[...])
    c, b, v = p[:, :d], p[:, d:2 * d], p[:, 2 * d:]
    u = c * v

    @pl.when(t == 0)
    def _():
        ubuf_ref[0:8, :] = jnp.zeros((8, d), F32)

    ubuf_ref[8:8 + tm, :] = u
    cw = cw_ref[...]
    y = cw[0:1] * ubuf_ref[6:6 + tm, :] + cw[1:2] * ubuf_ref[7:7 + tm, :] + cw[2:3] * u
    out = _dot((b * y).astype(BF16), wout_ref[...])
    y_ref[...] = x + _rms(out, g3_ref[...])
    last2 = ubuf_ref[6 + tm:8 + tm, :]
    st_ref[...] = last2
    ubuf_ref[6:8, :] = last2


def _conv_prompt(x, g2, g3, w_in, cw, w_out, batch, tm):
    m, d = x.shape
    t = m // batch
    nt = t // tm
    return pl.pallas_call(
        _conv_prompt_kernel,
        grid=(batch, nt),
        in_specs=[
            pl.BlockSpec((tm, d), lambda b, i: (b * nt + i, 0)),
            pl.BlockSpec((1, d), lambda b, i: (0, 0)),
            pl.BlockSpec((1, d), lambda b, i: (0, 0)),
            pl.BlockSpec((d, 3 * d), lambda b, i: (0, 0)),
            pl.BlockSpec((CONV_W, d), lambda b, i: (0, 0)),
            pl.BlockSpec((d, d), lambda b, i: (0, 0)),
        ],
        out_specs=[
            pl.BlockSpec((tm, d), lambda b, i: (b * nt + i, 0)),
            pl.BlockSpec((None, CONV_W - 1, d), lambda b, i: (b, 0, 0)),
        ],
        out_shape=[
            jax.ShapeDtypeStruct((m, d), F32),
            jax.ShapeDtypeStruct((batch, CONV_W - 1, d), F32),
        ],
        scratch_shapes=[pltpu.VMEM((tm + 8, d), F32)],
        compiler_params=_params("arbitrary", "arbitrary"),
        name="conv_prompt",
    )(x, g2, g3, w_in, cw, w_out)


def _conv_sample_kernel(x_ref, st_ref, g2_ref, g3_ref, win_ref, cw_ref, wout_ref, y_ref, nst_ref):
    d = x_ref.shape[1]
    x = x_ref[...]
    h = _rms(x, g2_ref[...]).astype(BF16)
    p = _dot(h, win_ref[...])
    c, b, v = p[:, :d], p[:, d:2 * d], p[:, 2 * d:]
    u = c * v
    s0, s1 = st_ref[:, :d], st_ref[:, d:]
    cw = cw_ref[...]
    y = cw[0:1] * s0 + cw[1:2] * s1 + cw[2:3] * u
    out = _dot((b * y).astype(BF16), wout_ref[...])
    y_ref[...] = x + _rms(out, g3_ref[...])
    nst_ref[:, :d] = s1
    nst_ref[:, d:] = u


def _conv_sample(x, state, g2, g3, w_in, cw, w_out):
    m, d = x.shape
    return pl.pallas_call(
        _conv_sample_kernel,
        out_shape=[
            jax.ShapeDtypeStruct((m, d), F32),
            jax.ShapeDtypeStruct((m, (CONV_W - 1) * d), F32),
        ],
        compiler_params=pltpu.CompilerParams(vmem_limit_bytes=VMEM_LIMIT_BYTES),
        name="conv_sample",
    )(x, state, g2, g3, w_in, cw, w_out)


ATTN_BLOCK = 256


def _store_transposed(wt_ref, h, out_ref):
    res = _dot_nt(wt_ref[...], h).astype(out_ref.dtype)
    for c in range(out_ref.shape[0]):
        out_ref[c] = res[:, c * ATTN_BLOCK:(c + 1) * ATTN_BLOCK]


def _diff_proj_kernel(x_ref, g_ref, w_ref, *refs):
    h = _rms(x_ref[...], g_ref[...]).astype(BF16)
    p = _dot(h, w_ref[...])
    if len(refs) == 5:
        wvt_ref, q_ref, kv_ref, kvb_ref, vt_ref = refs
        _store_transposed(wvt_ref, h, vt_ref)
    else:
        q_ref, kv_ref, kvb_ref = refs
    nq = q_ref.shape[1]
    q_ref[...] = p[:, :nq] * QK_SCALE
    kv = p[:, nq:]
    kv_ref[...] = kv
    kvb_ref[...] = kv.astype(BF16)


def _diff_proj(x, g, w, tm, wvt=None):
    m, d = x.shape
    n = w.shape[1]
    nq = DIFF_KV_HEADS * DIFF_REP * 2 * HEAD_DIM
    nkv = n - nq
    row = lambda i: (i, 0)
    in_specs = [pl.BlockSpec((tm, d), row), _resident((1, d)), _resident((d, n))]
    out_specs = [pl.BlockSpec((tm, nq), row), pl.BlockSpec((tm, nkv), row), pl.BlockSpec((tm, nkv), row)]
    out_shape = [jax.ShapeDtypeStruct((m, nq), F32), jax.ShapeDtypeStruct((m, nkv), F32),
                 jax.ShapeDtypeStruct((m, nkv), BF16)]
    args = [x, g, w]
    if wvt is not None:
        nb = tm // ATTN_BLOCK
        in_specs.append(_resident(wvt.shape))
        args.append(wvt)
        out_specs.append(pl.BlockSpec((nb, wvt.shape[0], ATTN_BLOCK), lambda i: (i, 0, 0)))
        out_shape.append(jax.ShapeDtypeStruct((m // ATTN_BLOCK, wvt.shape[0], ATTN_BLOCK), BF16))
    return pl.pallas_call(
        _diff_proj_kernel,
        grid=(m // tm,),
        in_specs=in_specs,
        out_specs=out_specs,
        out_shape=out_shape,
        compiler_params=_params("arbitrary"),
        name="diff_proj",
    )(*args)


def _diff_value_weights_t(w_in):
    nq = DIFF_KV_HEADS * DIFF_REP * 2 * HEAD_DIM
    kv = w_in[:, nq:].reshape(w_in.shape[0], DIFF_KV_HEADS, 4 * HEAD_DIM)
    v = kv[:, :, 2 * HEAD_DIM:].reshape(w_in.shape[0], DIFF_KV_HEADS * 2 * HEAD_DIM)
    return v.T.astype(BF16)


def _diff_lambda(lam_ref, lam_init):
    lp = lam_ref[...]
    a = jnp.sum(lp[0:1] * lp[1:2], axis=-1, keepdims=True)
    b = jnp.sum(lp[2:3] * lp[3:4], axis=-1, keepdims=True)
    return jnp.exp(a) - jnp.exp(b) + lam_init


def _diff_finish(o1, o2, lam, sub_g, lam_init):
    dlt = o1 - lam * o2
    return _rms(dlt, sub_g) * (1.0 - lam_init)


def _online_step(heads, kv_k, v_t, bias, carry):
    return _online_update(_dot_nt(kv_k, heads), v_t, bias, carry)


def _online_update(s, v_t, bias, carry):
    m_prev, l_prev, acc_prev = carry
    if bias is not None:
        if jnp.ndim(bias) == 2:
            bias = jnp.concatenate([bias] * (s.shape[1] // bias.shape[1]), axis=1)
        s = s + bias
    m_next = jnp.maximum(m_prev, jnp.max(s, axis=0, keepdims=True))
    p = jnp.exp(s - m_next)
    alpha = jnp.exp(m_prev - m_next)
    return (m_next, alpha * l_prev + jnp.sum(p, axis=0, keepdims=True),
            alpha * acc_prev + _dot(v_t, p.astype(BF16)))


def _online_init(n_heads, tq):
    w = n_heads * tq
    return (jnp.full((1, w), NEG, F32), jnp.zeros((1, w), F32), jnp.zeros((LANES, w), F32))


def _online_result(carry, h):
    _, l, acc = carry
    tq = ATTN_BLOCK
    return acc[:, h * tq:(h + 1) * tq] / jnp.maximum(l[:, h * tq:(h + 1) * tq], TINY)


def _attend_bias(ok):
    return jnp.where(ok, 0.0, NEG)


def _stacked_positions(first, rows):
    r = lax.broadcasted_iota(jnp.int32, (4 * rows, 1), 0)
    return first + (r & (rows - 1))


def _causal_t(tq):
    return lax.broadcasted_iota(jnp.int32, (tq, tq), 0) <= lax.broadcasted_iota(jnp.int32, (tq, tq), 1)


def _diff_attn_kernel(lam_ref, subt_ref, q_ref, k_ref, vt_ref, o_ref, *, lam_init):
    i = pl.program_id(2)
    tq = q_ref.shape[0]
    q = q_ref[...]
    qa, qb = q[:, :LANES], q[:, LANES:]
    heads = jnp.concatenate(
        [_low_half(qa), _low_half(qb), _high_half(qa), _high_half(qb)], axis=0).astype(BF16)

    def step(j, bias, carry):
        keys = k_ref[pl.ds(pl.multiple_of(j * tq, tq), tq), :]
        return _online_step(heads, keys, vt_ref[j], bias, carry)

    carry = lax.fori_loop(0, i, lambda j, c: step(j, None, c), _online_init(4, tq))
    carry = step(i, _attend_bias(_causal_t(tq)), carry)
    lam = _diff_lambda(lam_ref, lam_init)
    for r in range(DIFF_REP):
        dlt = _online_result(carry, r) - lam * _online_result(carry, DIFF_REP + r)
        ms = jnp.mean(dlt * dlt, axis=0, keepdims=True)
        res = dlt * lax.rsqrt(ms + EPS) * subt_ref[...] * (1.0 - lam_init)
        o_ref[:, r * LANES:(r + 1) * LANES] = res.T.astype(o_ref.dtype)


def _diff_attn(q, kvb, vt, lam_p, sub_g, lam_init, batch):
    m, nq = q.shape
    tq = ATTN_BLOCK
    t = m // batch
    nt = t // tq
    gw = nq // DIFF_KV_HEADS
    return pl.pallas_call(
        functools.partial(_diff_attn_kernel, lam_init=lam_init),
        grid=(batch, DIFF_KV_HEADS, nt),
        in_specs=[
            pl.BlockSpec((4, HEAD_DIM), lambda b, g, i: (0, 0)),
            pl.BlockSpec((2 * HEAD_DIM, 1), lambda b, g, i: (0, 0)),
            pl.BlockSpec((tq, gw), lambda b, g, i: (b * nt + i, g)),
            pl.BlockSpec((t, LANES), lambda b, g, i: (b, 2 * g)),
            pl.BlockSpec((nt, LANES, tq), lambda b, g, i: (b, g, 0)),
        ],
        out_specs=pl.BlockSpec((tq, gw), lambda b, g, i: (b * nt + i, g)),
        out_shape=jax.ShapeDtypeStruct((m, nq), BF16),
        compiler_params=_params("arbitrary", "arbitrary", "arbitrary"),
        name="diff_attn",
    )(lam_p, sub_g.reshape(-1, 1), q, kvb, vt)


GROUPS = 4
ROWS_PER_KEY = 8


def _interleaved_pages(cache):
    nl, n_p, s, g, w = cache.shape
    x = cache.reshape(nl, n_p, s, g, w // LANES, LANES).transpose(0, 1, 2, 4, 3, 5)
    return x.reshape(nl, n_p, s * (w // LANES) * g, LANES)


def _rows_per_group(x, width, offset):
    parts = []
    for g in range(GROUPS):
        piece = x[:, g * width + offset:g * width + offset + LANES]
        parts.append(jnp.concatenate([piece] * 4, axis=0))
    return jnp.concatenate(parts, axis=0)


def _diff_sample_kernel(pt_ref, lam_ref, sub_ref, q_ref, kvn_ref, *refs, lam_init, n_pages):
    del pt_ref
    pages = refs[:n_pages]
    o_ref = refs[n_pages]
    lam = _diff_lambda(lam_ref, lam_init)
    sub_g = sub_ref[...]
    gw = 4 * HEAD_DIM
    rows = []
    for g in range(GROUPS):
        qg = q_ref[:, g * gw:(g + 1) * gw]
        qa, qb = qg[:, :LANES], qg[:, LANES:]
        rows += [_low_half(qa), _low_half(qb), _high_half(qa), _high_half(qb)]
    q16 = jnp.concatenate(rows, axis=0).astype(BF16)
    new = kvn_ref[...].astype(BF16).astype(F32)
    k_new = _rows_per_group(new, gw, 0)
    v_new = _rows_per_group(new, gw, LANES)
    xs = [pages[p][...].astype(BF16) for p in range(n_pages)]
    psz = xs[0].shape[0]
    s = jnp.concatenate([_dot_nt(q16, x) for x in xs], axis=1)
    row_g = lax.shift_right_logical(lax.broadcasted_iota(jnp.int32, (4 * GROUPS, 1), 0), 2)
    col = lax.broadcasted_iota(jnp.int32, (1, s.shape[1]), 1)
    own = (col & (ROWS_PER_KEY - 1)) == row_g
    s = jnp.where(own, s, NEG)
    s_new = jnp.sum(q16.astype(F32) * k_new, axis=-1, keepdims=True)
    mx = jnp.maximum(jnp.max(s, axis=-1, keepdims=True), s_new)
    p = jnp.where(own, jnp.exp(s - mx), 0.0)
    p_new = jnp.exp(s_new - mx)
    den = jnp.sum(p, axis=-1, keepdims=True) + p_new
    acc = p_new.astype(BF16).astype(F32) * v_new
    for pi, x in enumerate(xs):
        pv = pltpu.roll(p[:, pi * psz:(pi + 1) * psz], GROUPS, 1).astype(BF16)
        acc = acc + _dot(pv, x)
    o = acc / jnp.maximum(den, TINY)
    for g in range(GROUPS):
        for r in range(DIFF_REP):
            a = 4 * g + r
            res = _diff_finish(o[a:a + 1], o[a + 2:a + 3], lam, sub_g, lam_init)
            o_ref[:, g * gw + r * LANES:g * gw + (r + 1) * LANES] = res.astype(o_ref.dtype)


def _page_spec(shape_tail, layer, p):
    return pl.BlockSpec((None, None) + shape_tail, lambda b, pt: (layer, pt[b, p], 0, 0))


def _diff_sample(q, kv_new, cache, layer, page_table, lam_p, sub_g, lam_init):
    m, nq = q.shape
    n_pages = page_table.shape[1]
    cache = _interleaved_pages(cache)
    tail = cache.shape[2:]
    row = lambda b, pt: (b, 0, 0)
    grid_spec = pltpu.PrefetchScalarGridSpec(
        num_scalar_prefetch=1,
        grid=(m,),
        in_specs=[
            pl.BlockSpec((4, HEAD_DIM), lambda b, pt: (0, 0)),
            pl.BlockSpec((1, 2 * HEAD_DIM), lambda b, pt: (0, 0)),
            pl.BlockSpec((None, 1, nq), row),
            pl.BlockSpec((None, 1, kv_new.shape[1]), row),
        ] + [_page_spec(tail, layer, p) for p in range(n_pages)],
        out_specs=pl.BlockSpec((None, 1, nq), row),
    )
    out = pl.pallas_call(
        functools.partial(_diff_sample_kernel, lam_init=lam_init, n_pages=n_pages),
        grid_spec=grid_spec,
        out_shape=jax.ShapeDtypeStruct((m, 1, nq), BF16),
        compiler_params=_params("arbitrary"),
        name="diff_sample",
    )(page_table, lam_p, sub_g, q[:, None, :], kv_new[:, None, :], *([cache] * n_pages))
    return out[:, 0, :]


def _nsa_proj_kernel(x_ref, g_ref, w_ref, wc_ref, wt_ref, q_ref, rows_ref, rowsb_ref, win_ref, winb_ref,
                     gates_ref, ce_ref, co_ref, kvt_ref):
    tm = x_ref.shape[0]
    h = _rms(x_ref[...], g_ref[...]).astype(BF16)
    p = _dot(h, w_ref[...])
    if kvt_ref is not None:
        _store_transposed(wt_ref, h, kvt_ref)
    nq, nr, nw = q_ref.shape[1], rows_ref.shape[1], win_ref.shape[1]
    q_ref[...] = p[:, :nq] * QK_SCALE
    rows = p[:, nq:nq + nr]
    rows_ref[...] = rows
    rowsb_ref[...] = rows.astype(BF16)
    win = p[:, nq + nr:nq + nr + nw]
    win_ref[...] = win
    winb_ref[...] = win.astype(BF16)
    gl = p[:, nq + nr + nw:]
    gates_ref[...] = 1.0 / (1.0 + jnp.exp(-gl))
    if ce_ref is not None:
        wc = wc_ref[...]
        gw = nr // NSA_GROUPS
        for g in range(NSA_GROUPS):
            kc = rows[:, g * gw:g * gw + LANES].reshape(tm // SEL_BLOCK, SEL_BLOCK, LANES) * wc[None]
            ce_ref[:, g * LANES:(g + 1) * LANES] = jnp.sum(kc[:, :CMP_BLOCK], axis=1)
            co_ref[:, g * LANES:(g + 1) * LANES] = jnp.sum(kc[:, CMP_BLOCK:], axis=1)


def _nsa_proj_kernel_nocmp(x_ref, g_ref, w_ref, q_ref, rows_ref, rowsb_ref, win_ref, winb_ref, gates_ref):
    _nsa_proj_kernel(x_ref, g_ref, w_ref, None, None, q_ref, rows_ref, rowsb_ref, win_ref, winb_ref,
                     gates_ref, None, None, None)


_NSA_NQ = NSA_GROUPS * NSA_HPG * HEAD_DIM
_NSA_NR = NSA_GROUPS * 4 * HEAD_DIM
_NSA_NW = NSA_GROUPS * 2 * HEAD_DIM
_NSA_NG = NSA_GROUPS * LANES


def _nsa_proj(x, g, w, wc, wt, tm, compress):
    m, d = x.shape
    n = w.shape[1]
    row = lambda i: (i, 0)
    in_specs = [pl.BlockSpec((tm, d), row), _resident((1, d)), _resident((d, n))]
    widths = [(_NSA_NQ, F32), (_NSA_NR, F32), (_NSA_NR, BF16), (_NSA_NW, F32), (_NSA_NW, BF16), (_NSA_NG, F32)]
    out_specs = [pl.BlockSpec((tm, wd), row) for wd, _ in widths]
    out_shape = [jax.ShapeDtypeStruct((m, wd), dt) for wd, dt in widths]
    args = [x, g, w]
    kern = _nsa_proj_kernel_nocmp
    if compress:
        nb = tm // SEL_BLOCK
        in_specs += [_resident((SEL_BLOCK, LANES)), _resident(wt.shape)]
        args += [wc, wt]
        out_specs += [pl.BlockSpec((nb, NSA_GROUPS * LANES), row)] * 2
        out_shape += [jax.ShapeDtypeStruct((m // SEL_BLOCK, NSA_GROUPS * LANES), F32)] * 2
        out_specs.append(pl.BlockSpec((tm // ATTN_BLOCK, wt.shape[0], ATTN_BLOCK), lambda i: (i, 0, 0)))
        out_shape.append(jax.ShapeDtypeStruct((m // ATTN_BLOCK, wt.shape[0], ATTN_BLOCK), BF16))
        kern = _nsa_proj_kernel
    return pl.pallas_call(
        kern,
        grid=(m // tm,),
        in_specs=in_specs,
        out_specs=out_specs,
        out_shape=out_shape,
        compiler_params=_params("arbitrary"),
        name="nsa_proj",
    )(*args)


def _nsa_queries(q):
    qa, qb = q[:, :LANES], q[:, LANES:]
    parts = [_low_half(qa), _low_half(_swap_halves(qa)), _low_half(qb), _low_half(_swap_halves(qb))]
    return jnp.concatenate(parts, axis=0).astype(BF16)


def _select_blocks(imp, n_keep):
    n = imp.shape[1]
    sidx = lax.broadcasted_iota(jnp.int32, imp.shape, 1)
    cnt = jnp.zeros(imp.shape, F32)
    for s2 in range(n):
        col = imp[:, s2:s2 + 1]
        beats = (col > imp) | ((col == imp) & (sidx > s2))
        cnt = cnt + jnp.where(beats, 1.0, 0.0)
    return jnp.where(cnt < n_keep, 1.0, 0.0)


def _head_rows(x, rows):
    return [x[h * rows:(h + 1) * rows] for h in range(NSA_HPG)]


def _nsa_gate_mix(gates, o_c, o_s, o_w):
    mixed = []
    for h in range(NSA_HPG):
        mixed.append(gates[:, 3 * h:3 * h + 1] * o_c[h] + gates[:, 3 * h + 1:3 * h + 2] * o_s[h]
                     + gates[:, 3 * h + 2:3 * h + 3] * o_w[h])
    lane = lax.broadcasted_iota(jnp.int32, mixed[0].shape, 1)
    pair0 = jnp.where(lane < HEAD_DIM, _swap_halves(mixed[0]), mixed[1])
    pair1 = jnp.where(lane < HEAD_DIM, _swap_halves(mixed[2]), mixed[3])
    return pair0, pair1


def _select_blocks_t(imp_t, n_keep):
    n = imp_t.shape[0]
    sidx = lax.broadcasted_iota(jnp.int32, imp_t.shape, 0)
    cnt = jnp.zeros(imp_t.shape, F32)
    for s2 in range(n):
        row = imp_t[s2:s2 + 1, :]
        beats = (row > imp_t) | ((row == imp_t) & (sidx > s2))
        cnt = cnt + jnp.where(beats, 1.0, 0.0)
    return jnp.where(cnt < n_keep, 1.0, 0.0)


def _nsa_attn_kernel(q_ref, gates_ref, ce_ref, co_ref, rows_ref, win_ref, selt_ref, wint_ref, emat_ref, o_ref):
    i = pl.program_id(2)
    tq = q_ref.shape[0]
    qs = _nsa_queries(q_ref[...])
    tpos4 = _stacked_positions(i * tq, tq)
    kpos0 = lax.broadcasted_iota(jnp.int32, (1, tq), 1)

    n_half = ce_ref.shape[0]
    kvc = jnp.concatenate([ce_ref[...], co_ref[...]], axis=0).astype(BF16)
    cidx = lax.broadcasted_iota(jnp.int32, (1, 2 * n_half), 1)
    cblk = jnp.where(cidx < n_half, 2 * cidx, 2 * (cidx - n_half) + 1)
    mask_c = ((cblk + 1) * CMP_BLOCK - 1) <= tpos4
    p_c = _masked_softmax(_dot_nt(qs, kvc), mask_c)
    o_c = _dot(p_c.astype(BF16), kvc)

    tpos_t = i * tq + kpos0
    tpos4_t = jnp.concatenate([tpos_t] * 4, axis=1)
    ridx = lax.broadcasted_iota(jnp.int32, (2 * n_half, 1), 0)
    rblk = jnp.where(ridx < n_half, 2 * ridx, 2 * (ridx - n_half) + 1)
    mask_t = ((rblk + 1) * CMP_BLOCK - 1) <= tpos4_t
    s_t = jnp.where(mask_t, _dot_nt(kvc, qs), NEG)
    p_t = jnp.where(mask_t, jnp.exp(s_t - jnp.max(s_t, axis=0, keepdims=True)), 0.0)
    p_t = p_t / jnp.maximum(jnp.sum(p_t, axis=0, keepdims=True), TINY)
    psum = p_t[:, 0:tq] + p_t[:, tq:2 * tq] + p_t[:, 2 * tq:3 * tq] + p_t[:, 3 * tq:]
    imp = psum[:n_half] + psum[n_half:]
    sidx = lax.broadcasted_iota(jnp.int32, (n_half, 1), 0)
    forced = (sidx == lax.shift_right_logical(tpos_t, int(math.log2(SEL_BLOCK)))) | (sidx == 0)
    valid = sidx * SEL_BLOCK <= tpos_t
    imp = jnp.where(forced, FORCE_SCORE, jnp.where(valid, imp, NEG))
    sel_t = _select_blocks_t(imp, min(N_SEL, n_half)).astype(BF16)

    heads = qs
    causal = _causal_t(tq)

    def transposed(carry):
        return [_online_result(carry, h).T for h in range(NSA_HPG)]

    def sel_step(j, extra_ok, carry):
        keys = rows_ref[pl.ds(pl.multiple_of(j * tq, tq), tq), LANES:]
        ok = _dot(emat_ref[j], sel_t) > 0.5
        if extra_ok is not None:
            ok = ok & extra_ok
        return _online_step(heads, keys, selt_ref[j], _attend_bias(ok), carry)

    carry = lax.fori_loop(0, i, lambda j, c: sel_step(j, None, c), _online_init(NSA_HPG, tq))
    o_s = transposed(sel_step(i, causal, carry))

    n_back = WINDOW // tq
    carry = _online_init(NSA_HPG, tq)
    for dj in range(n_back, -1, -1):
        j = i - dj
        jc = jnp.maximum(j, 0)
        keys = win_ref[pl.ds(pl.multiple_of(jc * tq, tq), tq), :]
        if dj == n_back:
            kpos = j * tq + lax.broadcasted_iota(jnp.int32, (tq, tq), 0)
            qpos = i * tq + lax.broadcasted_iota(jnp.int32, (tq, tq), 1)
            bias = _attend_bias((kpos > qpos - WINDOW) & (kpos >= 0))
        elif dj == 0:
            bias = _attend_bias(causal)
        else:
            bias = jnp.where(j >= 0, 0.0, NEG)
        carry = _online_step(heads, keys, wint_ref[jc], bias, carry)
    o_w = transposed(carry)

    pair0, pair1 = _nsa_gate_mix(gates_ref[...], _head_rows(o_c, tq), o_s, o_w)
    o_ref[:, :LANES] = pair0.astype(o_ref.dtype)
    o_ref[:, LANES:] = pair1.astype(o_ref.dtype)


def _sel_expand_matrix_t(n_blocks, n_keys, tk):
    key = lax.broadcasted_iota(jnp.int32, (n_keys, n_blocks), 0)
    blk = lax.broadcasted_iota(jnp.int32, (n_keys, n_blocks), 1)
    e = jnp.where(key // SEL_BLOCK == blk, 1.0, 0.0).astype(BF16)
    return e.reshape(n_keys // tk, tk, n_blocks)


def _nsa_attn(q, gates, ce, co, rowsb, winb, kvt, batch):
    m, nq = q.shape
    tq = ATTN_BLOCK
    t = m // batch
    nt = t // tq
    n_half = t // SEL_BLOCK
    gw = nq // NSA_GROUPS
    assert WINDOW % tq == 0
    emat = _sel_expand_matrix_t(n_half, t, tq)
    return pl.pallas_call(
        _nsa_attn_kernel,
        grid=(batch, NSA_GROUPS, nt),
        in_specs=[
            pl.BlockSpec((tq, gw), lambda b, g, i: (b * nt + i, g)),
            pl.BlockSpec((tq, LANES), lambda b, g, i: (b * nt + i, g)),
            pl.BlockSpec((n_half, LANES), lambda b, g, i: (b, g)),
            pl.BlockSpec((n_half, LANES), lambda b, g, i: (b, g)),
            pl.BlockSpec((t, gw), lambda b, g, i: (b, g)),
            pl.BlockSpec((t, LANES), lambda b, g, i: (b, g)),
            pl.BlockSpec((nt, LANES, tq), lambda b, g, i: (b, g, 0)),
            pl.BlockSpec((nt, LANES, tq), lambda b, g, i: (b, NSA_GROUPS + g, 0)),
            _resident((nt, tq, n_half)),
        ],
        out_specs=pl.BlockSpec((tq, gw), lambda b, g, i: (b * nt + i, g)),
        out_shape=jax.ShapeDtypeStruct((m, nq), BF16),
        compiler_params=_params("arbitrary", "arbitrary", "arbitrary"),
        name="nsa_attn",
    )(q, gates, ce, co, rowsb, winb, kvt, kvt, emat)


def _softmax_with_new(s, ok, s_new):
    s = jnp.where(ok, s, NEG)
    mx = jnp.maximum(jnp.max(s, axis=-1, keepdims=True), s_new)
    p = jnp.where(ok, jnp.exp(s - mx), 0.0)
    p_new = jnp.exp(s_new - mx)
    return p, p_new, jnp.sum(p, axis=-1, keepdims=True) + p_new


def _nsa_sample_kernel(pt_ref, q_ref, gates_ref, rown_ref, winn_ref, wrep_ref, wst_ref, *refs, n_pages):
    del pt_ref
    pages = refs[:n_pages]
    o_ref, wout_ref, cmp_ref = refs[n_pages:n_pages + 3]
    gw = 4 * HEAD_DIM
    psz = pages[0].shape[0]
    n_wrows = wst_ref.shape[0]
    n_old = n_wrows // GROUPS
    n_half = cmp_ref.shape[0]
    q16 = jnp.concatenate([_nsa_queries(q_ref[:, g * gw:(g + 1) * gw]) for g in range(GROUPS)], axis=0)
    q16f = q16.astype(F32)
    row_g = lax.shift_right_logical(lax.broadcasted_iota(jnp.int32, (4 * GROUPS, 1), 0), 2)
    xs = [pages[p][...] for p in range(n_pages)]

    wrep = wrep_ref[...]
    nb = psz // ROWS_PER_KEY // CMP_BLOCK
    for p, x in enumerate(xs):
        blocks = jnp.sum((x * wrep).reshape(nb, CMP_BLOCK, ROWS_PER_KEY, LANES), axis=1)
        cmp_ref[p * nb // 2:(p + 1) * nb // 2] = blocks.reshape(nb // 2, 2, ROWS_PER_KEY, LANES)

    o_c, sel16 = [], []
    for g in range(GROUPS):
        q8 = jnp.concatenate([q16[4 * g:4 * g + 4]] * 2, axis=0)
        kvc = jnp.concatenate([cmp_ref[:, 0, g, :], cmp_ref[:, 1, g, :]], axis=0).astype(BF16)
        s_c = _dot_nt(q8, kvc)
        p_c = _masked_softmax(s_c, jnp.full(s_c.shape, True))
        o_c.append(_dot(p_c.astype(BF16), kvc)[0:4])
        psum = p_c[0:1] + p_c[1:2] + p_c[2:3] + p_c[3:4]
        imp = psum[:, :n_half] + psum[:, n_half:]
        sidx = lax.broadcasted_iota(jnp.int32, (1, n_half), 1)
        imp = jnp.where(sidx == 0, FORCE_SCORE, imp)
        sel16.append(_select_blocks(jnp.concatenate([imp] * 4, axis=0), N_SEL - 1))
    o_c = jnp.concatenate(o_c, axis=0)
    sel16 = jnp.concatenate(sel16, axis=0)

    xb = [x.astype(BF16) for x in xs]
    s_s = jnp.concatenate([_dot_nt(q16, x) for x in xb], axis=1)
    col = lax.broadcasted_iota(jnp.int32, (1, psz), 1)
    own = (col & (ROWS_PER_KEY - 1)) == (row_g + GROUPS)
    first_half = col < psz // 2
    ok = []
    for p in range(n_pages):
        chosen = jnp.where(first_half, sel16[:, 2 * p:2 * p + 1], sel16[:, 2 * p + 1:2 * p + 2]) > 0.5
        ok.append(own & chosen)
    new_rows = rown_ref[...].astype(BF16).astype(F32)
    ks_new = _rows_per_group(new_rows, gw, LANES)
    s_new = jnp.sum(q16f * ks_new, axis=-1, keepdims=True)
    p_s, p_new, den = _softmax_with_new(s_s, jnp.concatenate(ok, axis=1), s_new)
    acc = p_new.astype(BF16).astype(F32) * ks_new
    for pi, x in enumerate(xb):
        acc = acc + _dot(p_s[:, pi * psz:(pi + 1) * psz].astype(BF16), x)
    o_s = acc / jnp.maximum(den, TINY)

    wb = wst_ref[...].astype(BF16)
    wcol = lax.broadcasted_iota(jnp.int32, (1, n_wrows), 1)
    ok_w = ((wcol & (GROUPS - 1)) == row_g) & (lax.shift_right_logical(wcol, 2) > n_old - WINDOW)
    w_new = _rows_per_group(winn_ref[...].astype(BF16).astype(F32), LANES, 0)
    s_wn = jnp.sum(q16f * w_new, axis=-1, keepdims=True)
    p_w, p_wn, den = _softmax_with_new(_dot_nt(q16, wb), ok_w, s_wn)
    acc = p_wn.astype(BF16).astype(F32) * w_new + _dot(p_w.astype(BF16), wb)
    o_w = acc / jnp.maximum(den, TINY)

    for g in range(GROUPS):
        sl = slice(4 * g, 4 * g + 4)
        pair0, pair1 = _nsa_gate_mix(gates_ref[:, g * LANES:(g + 1) * LANES], _head_rows(o_c[sl], 1),
                                     _head_rows(o_s[sl], 1), _head_rows(o_w[sl], 1))
        o_ref[:, g * gw:g * gw + LANES] = pair0.astype(o_ref.dtype)
        o_ref[:, g * gw + LANES:(g + 1) * gw] = pair1.astype(o_ref.dtype)

    wout_ref[0:n_wrows - GROUPS, :] = wst_ref[GROUPS:n_wrows, :]
    wout_ref[n_wrows - GROUPS:n_wrows, :] = jnp.concatenate(
        [winn_ref[:, g * LANES:(g + 1) * LANES] for g in range(GROUPS)], axis=0)


def _nsa_sample(q, gates, rows_new, win_new, wc32, cache, win_state, layer, page_table):
    m, nq = q.shape
    n_pages = page_table.shape[1]
    cache = _interleaved_pages(cache)
    tail = cache.shape[2:]
    keys_per_page = tail[0] // ROWS_PER_KEY
    n_cmp = n_pages * keys_per_page // CMP_BLOCK
    nl, _, n_old, wg, ww = win_state.shape
    win_rows = win_state.reshape(nl, m, n_old * wg, ww)
    wrep = jnp.repeat(jnp.tile(wc32, (keys_per_page // CMP_BLOCK, 1)), ROWS_PER_KEY, axis=0)
    row = lambda b, pt: (b, 0, 0)
    grid_spec = pltpu.PrefetchScalarGridSpec(
        num_scalar_prefetch=1,
        grid=(m,),
        in_specs=[
            pl.BlockSpec((None, 1, nq), row),
            pl.BlockSpec((None, 1, gates.shape[1]), row),
            pl.BlockSpec((None, 1, rows_new.shape[1]), row),
            pl.BlockSpec((None, 1, win_new.shape[1]), row),
            pl.BlockSpec(wrep.shape, lambda b, pt: (0, 0)),
            pl.BlockSpec((None, None, n_old * wg, ww), lambda b, pt: (layer, b, 0, 0)),
        ] + [_page_spec(tail, layer, p) for p in range(n_pages)],
        out_specs=[
            pl.BlockSpec((None, 1, nq), row),
            pl.BlockSpec((None, n_old * wg, ww), row),
        ],
        scratch_shapes=[pltpu.VMEM((n_cmp // 2, 2, ROWS_PER_KEY, LANES), F32)],
    )
    o, wout = pl.pallas_call(
        functools.partial(_nsa_sample_kernel, n_pages=n_pages),
        grid_spec=grid_spec,
        out_shape=[
            jax.ShapeDtypeStruct((m, 1, nq), BF16),
            jax.ShapeDtypeStruct((m, n_old * wg, ww), F32),
        ],
        compiler_params=_params("arbitrary"),
        name="nsa_sample",
    )(page_table, q[:, None, :], gates[:, None, :], rows_new[:, None, :], win_new[:, None, :], wrep,
      win_rows, *([cache] * n_pages))
    return o[:, 0, :], wout.reshape(m, n_old, wg, ww)


def _nsa_pair_weights_t(w):
    pair = 2 * HEAD_DIM
    sel = [w[:, _NSA_NQ + g * 2 * pair + pair:_NSA_NQ + (g + 1) * 2 * pair] for g in range(NSA_GROUPS)]
    win = w[:, _NSA_NQ + _NSA_NR:_NSA_NQ + _NSA_NR + _NSA_NW]
    return jnp.concatenate(sel + [win], axis=1).T


def _nsa_weight_layout(w_in):
    d = w_in.shape[0]
    kv0 = _NSA_NQ
    per_g = 6 * HEAD_DIM
    g0 = kv0 + NSA_GROUPS * per_g
    rows_idx = [kv0 + g * per_g + c for g in range(NSA_GROUPS) for c in range(4 * HEAD_DIM)]
    win_idx = [kv0 + g * per_g + 4 * HEAD_DIM + c for g in range(NSA_GROUPS) for c in range(2 * HEAD_DIM)]
    n_gate = NSA_HPG * 3
    gate_cols = []
    for g in range(NSA_GROUPS):
        gate_cols.append(w_in[:, g0 + g * n_gate:g0 + (g + 1) * n_gate])
        gate_cols.append(jnp.zeros((d, LANES - n_gate), w_in.dtype))
    return jnp.concatenate(
        [w_in[:, :kv0], w_in[:, jnp.array(rows_idx)], w_in[:, jnp.array(win_idx)]] + gate_cols, axis=1)


def kernel(x_prompt, x_sample, state_conv, cache_nsa, state_nsa_win, cache_diff, page_table, norm_g, ffn_w_in,
           ffn_w_out, conv_w_in, conv_w, conv_w_out, nsa_w_in, nsa_w_cmp, nsa_w_out, diff_w_in, diff_lambda,
           diff_subln_g, diff_w_out):
    batch, seq, d = x_prompt.shape
    dec_batch, dec_seq, _ = x_sample.shape
    assert dec_seq == 1
    depth = norm_g.shape[0]
    xp = x_prompt.reshape(batch * seq, d)
    xs = x_sample.reshape(dec_batch, d)
    tm = 512
    ffn_in = ffn_w_in.astype(BF16)
    ffn_out = ffn_w_out.astype(BF16)
    outs = {k: [] for k in ("conv_p", "conv_s", "nsa_p", "nsa_s", "win_p", "win_s", "diff_p", "diff_s")}
    for i in range(depth):
        g = [norm_g[i, k][None, :] for k in range(6)]
        kind, j = i % N_MIXERS, i // N_MIXERS
        xp = _ffn(xp, g[0], g[1], ffn_in, ffn_out, i, 0, tm)
        xs = _ffn(xs, g[0], g[1], ffn_in, ffn_out, i, 0, dec_batch)
        if kind == 0:
            w_in, w_out = conv_w_in[j].astype(BF16), conv_w_out[j].astype(BF16)
            xp, st = _conv_prompt(xp, g[2], g[3], w_in, conv_w[j], w_out, batch, tm)
            outs["conv_p"].append(st)
            xs, st = _conv_sample(xs, state_conv[j].reshape(dec_batch, -1), g[2], g[3], w_in, conv_w[j], w_out)
            outs["conv_s"].append(st.reshape(dec_batch, CONV_W - 1, d))
        elif kind == 1:
            w_in = _nsa_weight_layout(nsa_w_in[j]).astype(BF16)
            w_out = nsa_w_out[j].astype(BF16)
            wc32 = jnp.concatenate([nsa_w_cmp[j, 0], nsa_w_cmp[j, 1]], axis=1)
            wc = jnp.concatenate([wc32, wc32], axis=0)
            wt = _nsa_pair_weights_t(w_in)
            q, rows, rowsb, win, winb, gates, ce, co, kvt = _nsa_proj(xp, g[2], w_in, wc, wt, tm, True)
            o = _nsa_attn(q, gates, ce, co, rowsb, winb, kvt, batch)
            xp = _post(xp, o, w_out, g[3], tm)
            outs["nsa_p"].append(rows.reshape(batch, seq, NSA_GROUPS, 4 * HEAD_DIM))
            wkeep = min(WINDOW, seq)
            outs["win_p"].append(win.reshape(batch, seq, NSA_GROUPS, 2 * HEAD_DIM)[:, seq - wkeep:])
            q, rows, _, win, _, gates = _nsa_proj(xs, g[2], w_in, None, None, dec_batch, False)
            o, wout = _nsa_sample(q, gates, rows, win, wc32, cache_nsa, state_nsa_win, j, page_table)
            xs = _post(xs, o, w_out, g[3], dec_batch)
            outs["nsa_s"].append(rows.reshape(dec_batch, 1, NSA_GROUPS, 4 * HEAD_DIM))
            outs["win_s"].append(wout)
        else:
            lam_init = 0.8 - 0.6 * math.exp(-0.3 * i)
            w_in, w_out = diff_w_in[j].astype(BF16), diff_w_out[j].astype(BF16)
            sub_g = diff_subln_g[j][None, :]
            q, kv, kvb, vt = _diff_proj(xp, g[2], w_in, tm, _diff_value_weights_t(diff_w_in[j]))
            o = _diff_attn(q, kvb, vt, diff_lambda[j], sub_g, lam_init, batch)
            xp = _post(xp, o, w_out, g[3], tm)
            outs["diff_p"].append(kv.reshape(batch, seq, DIFF_KV_HEADS, 4 * HEAD_DIM))
            q, kv, _ = _diff_proj(xs, g[2], w_in, dec_batch)
            o = _diff_sample(q, kv, cache_diff, j, page_table, diff_lambda[j], sub_g, lam_init)
            xs = _post(xs, o, w_out, g[3], dec_batch)
            outs["diff_s"].append(kv.reshape(dec_batch, 1, DIFF_KV_HEADS, 4 * HEAD_DIM))
        xp = _ffn(xp, g[4], g[5], ffn_in, ffn_out, i, 1, tm)
        xs = _ffn(xs, g[4], g[5], ffn_in, ffn_out, i, 1, dec_batch)
    st = lambda k: jnp.stack(outs[k])
    return (xp.reshape(batch, seq, d), xs.reshape(dec_batch, 1, d), st("conv_p"), st("conv_s"), st("nsa_p"),
            st("nsa_s"), st("win_p"), st("win_s"), st("diff_p"), st("diff_s"))
```

```python
import functools
import math

import jax
import jax.numpy as jnp
from jax import lax
from jax.experimental import pallas as pl
from jax.experimental.pallas import tpu as pltpu

F32 = jnp.float32
BF16 = jnp.bfloat16

EPS = 1e-6
NEG = -1e30
TINY = 1e-30
N_MIXERS = 3
CONV_W = 3
HEAD_DIM = 64
NSA_GROUPS = 4
NSA_HPG = 4
CMP_BLOCK = 32
SEL_BLOCK = 64
N_SEL = 8
WINDOW = 512
FORCE_SCORE = 1e4
DIFF_KV_HEADS = 4
DIFF_REP = 2
QK_SCALE = HEAD_DIM ** -0.5

LANES = 128
VMEM_LIMIT_BYTES = 56 * 1024 * 1024

_NT = (((1,), (1,)), ((), ()))


def _params(*sem):
    return pltpu.CompilerParams(dimension_semantics=sem, vmem_limit_bytes=VMEM_LIMIT_BYTES)


def _rms(x, g):
    return x * lax.rsqrt(jnp.mean(x * x, axis=-1, keepdims=True) + EPS) * g


def _dot(a, b):
    return jnp.dot(a, b, preferred_element_type=F32)


def _dot_nt(a, b):
    return lax.dot_general(a, b, _NT, preferred_element_type=F32)


def _masked_softmax(s, mask):
    s = jnp.where(mask, s, NEG)
    m = jnp.max(s, axis=-1, keepdims=True)
    p = jnp.where(mask, jnp.exp(s - m), 0.0)
    return p / jnp.maximum(jnp.sum(p, axis=-1, keepdims=True), TINY)


def _low_half(x):
    lane = lax.broadcasted_iota(jnp.int32, x.shape, 1)
    return jnp.where(lane < HEAD_DIM, x, 0.0)


def _high_half(x):
    lane = lax.broadcasted_iota(jnp.int32, x.shape, 1)
    return jnp.where(lane >= HEAD_DIM, x, 0.0)


def _swap_halves(x):
    return pltpu.roll(x, HEAD_DIM, 1)


def _ffn_kernel(x_ref, gpre_ref, gpost_ref, wi_ref, wo_ref, o_ref):
    f = wo_ref.shape[0]
    x = x_ref[...]
    xn = _rms(x, gpre_ref[...]).astype(BF16)
    gate = _dot(xn, wi_ref[:, :f])
    up = _dot(xn, wi_ref[:, f:])
    act = (gate * (1.0 / (1.0 + jnp.exp(-gate))) * up).astype(BF16)
    o_ref[...] = x + 0.5 * _rms(_dot(act, wo_ref[...]), gpost_ref[...])


def _resident(shape):
    return pl.BlockSpec(shape, lambda *_: (0,) * len(shape), pipeline_mode=pl.Buffered(1))


def _ffn(x, g_pre, g_post, w_in, w_out, layer, half, tm):
    m, d = x.shape

    def picked(w):
        return pl.BlockSpec((None, None) + w.shape[2:], lambda i: (layer, half, 0, 0),
                            pipeline_mode=pl.Buffered(1))

    return pl.pallas_call(
        _ffn_kernel,
        grid=(m // tm,),
        in_specs=[
            pl.BlockSpec((tm, d), lambda i: (i, 0)),
            _resident((1, d)),
            _resident((1, d)),
            picked(w_in),
            picked(w_out),
        ],
        out_specs=pl.BlockSpec((tm, d), lambda i: (i, 0)),
        out_shape=jax.ShapeDtypeStruct((m, d), F32),
        compiler_params=_params("arbitrary"),
        name="half_ffn",
    )(x, g_pre, g_post, w_in, w_out)


def _post_kernel(x_ref, o_ref, w_ref, g_ref, y_ref):
    y = _dot(o_ref[...], w_ref[...])
    y_ref[...] = x_ref[...] + _rms(y, g_ref[...])


def _post(x, o, w, g, tm):
    m, d = x.shape
    k = o.shape[1]
    return pl.pallas_call(
        _post_kernel,
        grid=(m // tm,),
        in_specs=[
            pl.BlockSpec((tm, d), lambda i: (i, 0)),
            pl.BlockSpec((tm, k), lambda i: (i, 0)),
            pl.BlockSpec((k, d), lambda i: (0, 0)),
            pl.BlockSpec((1, d), lambda i: (0, 0)),
        ],
        out_specs=pl.BlockSpec((tm, d), lambda i: (i, 0)),
        out_shape=jax.ShapeDtypeStruct((m, d), F32),
        compiler_params=_params("arbitrary"),
        name="mixer_out",
    )(x, o, w, g)


def _conv_prompt_kernel(x_ref, g2_ref, g3_ref, win_ref, cw_ref, wout_ref, y_ref, st_ref, ubuf_ref):
    t = pl.program_id(1)
    tm, d = x_ref.shape
    x = x_ref[...]
    h = _rms(x, g2_ref[...]).astype(BF16)
    p = _dot(h, win_ref[...])
    c, b, v = p[:, :d], p[:, d:2 * d], p[:, 2 * d:]
    u = c * v

    @pl.when(t == 0)
    def _():
        ubuf_ref[0:8, :] = jnp.zeros((8, d), F32)

    ubuf_ref[8:8 + tm, :] = u
    cw = cw_ref[...]
    y = cw[0:1] * ubuf_ref[6:6 + tm, :] + cw[1:2] * ubuf_ref[7:7 + tm, :] + cw[2:3] * u
    out = _dot((b * y).astype(BF16), wout_ref[...])
    y_ref[...] = x + _rms(out, g3_ref[...])
    last2 = ubuf_ref[6 + tm:8 + tm, :]
    st_ref[...] = last2
    ubuf_ref[6:8, :] = last2


def _conv_prompt(x, g2, g3, w_in, cw, w_out, batch, tm):
    m, d = x.shape
    t = m // batch
    nt = t // tm
    return pl.pallas_call(
        _conv_prompt_kernel,
        grid=(batch, nt),
        in_specs=[
            pl.BlockSpec((tm, d), lambda b, i: (b * nt + i, 0)),
            pl.BlockSpec((1, d), lambda b, i: (0, 0)),
            pl.BlockSpec((1, d), lambda b, i: (0, 0)),
            pl.BlockSpec((d, 3 * d), lambda b, i: (0, 0)),
            pl.BlockSpec((CONV_W, d), lambda b, i: (0, 0)),
            pl.BlockSpec((d, d), lambda b, i: (0, 0)),
        ],
        out_specs=[
            pl.BlockSpec((tm, d), lambda b, i: (b * nt + i, 0)),
            pl.BlockSpec((None, CONV_W - 1, d), lambda b, i: (b, 0, 0)),
        ],
        out_shape=[
            jax.ShapeDtypeStruct((m, d), F32),
            jax.ShapeDtypeStruct((batch, CONV_W - 1, d), F32),
        ],
        scratch_shapes=[pltpu.VMEM((tm + 8, d), F32)],
        compiler_params=_params("arbitrary", "arbitrary"),
        name="conv_prompt",
    )(x, g2, g3, w_in, cw, w_out)


def _conv_sample_kernel(x_ref, st_ref, g2_ref, g3_ref, win_ref, cw_ref, wout_ref, y_ref, nst_ref):
    d = x_ref.shape[1]
    x = x_ref[...]
    h = _rms(x, g2_ref[...]).astype(BF16)
    p = _dot(h, win_ref[...])
    c, b, v = p[:, :d], p[:, d:2 * d], p[:, 2 * d:]
    u = c * v
    s0, s1 = st_ref[:, :d], st_ref[:, d:]
    cw = cw_ref[...]
    y = cw[0:1] * s0 + cw[1:2] * s1 + cw[2:3] * u
    out = _dot((b * y).astype(BF16), wout_ref[...])
    y_ref[...] = x + _rms(out, g3_ref[...])
    nst_ref[:, :d] = s1
    nst_ref[:, d:] = u


def _conv_sample(x, state, g2, g3, w_in, cw, w_out):
    m, d = x.shape
    return pl.pallas_call(
        _conv_sample_kernel,
        out_shape=[
            jax.ShapeDtypeStruct((m, d), F32),
            jax.ShapeDtypeStruct((m, (CONV_W - 1) * d), F32),
        ],
        compiler_params=pltpu.CompilerParams(vmem_limit_bytes=VMEM_LIMIT_BYTES),
        name="conv_sample",
    )(x, state, g2, g3, w_in, cw, w_out)


ATTN_BLOCK = 256


def _store_grouped(x, out_ref):
    rows = x.shape[0]
    w = x.shape[1] // GROUPS
    nc = w // LANES
    for g in range(GROUPS):
        for c in range(nc):
            out_ref[pl.ds(c * GROUPS + g, rows, stride=nc * GROUPS), :] = (
                x[:, g * w + c * LANES:g * w + (c + 1) * LANES])


def _grouped_view(o, lead, w):
    nc = w // LANES
    o = o.reshape(*lead, nc, GROUPS, LANES)
    return jnp.swapaxes(o, -3, -2).reshape(*lead, GROUPS, w)


def _store_transposed(wt_ref, h, out_ref):
    res = _dot_nt(wt_ref[...], h).astype(out_ref.dtype)
    for c in range(out_ref.shape[0]):
        out_ref[c] = res[:, c * ATTN_BLOCK:(c + 1) * ATTN_BLOCK]


def _diff_proj_kernel(x_ref, g_ref, w_ref, *refs):
    h = _rms(x_ref[...], g_ref[...]).astype(BF16)
    p = _dot(h, w_ref[...])
    prompt = len(refs) == 5
    if prompt:
        wvt_ref, q_ref, kv_ref, kvb_ref, vt_ref = refs
        _store_transposed(wvt_ref, h, vt_ref)
    else:
        q_ref, kv_ref, kvb_ref = refs
    nq = q_ref.shape[1]
    q_ref[...] = p[:, :nq] * QK_SCALE
    kv = p[:, nq:]
    if prompt:
        _store_grouped(kv, kv_ref)
    else:
        kv_ref[...] = kv
    kvb_ref[...] = kv.astype(BF16)


def _diff_proj(x, g, w, tm, wvt=None):
    m, d = x.shape
    n = w.shape[1]
    nq = DIFF_KV_HEADS * DIFF_REP * 2 * HEAD_DIM
    nkv = n - nq
    row = lambda i: (i, 0)
    in_specs = [pl.BlockSpec((tm, d), row), _resident((1, d)), _resident((d, n))]
    kv_rows = nkv // LANES if wvt is not None else 1
    out_specs = [pl.BlockSpec((tm, nq), row), pl.BlockSpec((tm * kv_rows, nkv // kv_rows), row),
                 pl.BlockSpec((tm, nkv), row)]
    out_shape = [jax.ShapeDtypeStruct((m, nq), F32), jax.ShapeDtypeStruct((m * kv_rows, nkv // kv_rows), F32),
                 jax.ShapeDtypeStruct((m, nkv), BF16)]
    args = [x, g, w]
    if wvt is not None:
        nb = tm // ATTN_BLOCK
        in_specs.append(_resident(wvt.shape))
        args.append(wvt)
        out_specs.append(pl.BlockSpec((nb, wvt.shape[0], ATTN_BLOCK), lambda i: (i, 0, 0)))
        out_shape.append(jax.ShapeDtypeStruct((m // ATTN_BLOCK, wvt.shape[0], ATTN_BLOCK), BF16))
    return pl.pallas_call(
        _diff_proj_kernel,
        grid=(m // tm,),
        in_specs=in_specs,
        out_specs=out_specs,
        out_shape=out_shape,
        compiler_params=_params("arbitrary"),
        name="diff_proj",
    )(*args)


def _diff_value_weights_t(w_in):
    nq = DIFF_KV_HEADS * DIFF_REP * 2 * HEAD_DIM
    kv = w_in[:, nq:].reshape(w_in.shape[0], DIFF_KV_HEADS, 4 * HEAD_DIM)
    v = kv[:, :, 2 * HEAD_DIM:].reshape(w_in.shape[0], DIFF_KV_HEADS * 2 * HEAD_DIM)
    return v.T.astype(BF16)


def _diff_lambda(lam_ref, lam_init):
    lp = lam_ref[...]
    a = jnp.sum(lp[0:1] * lp[1:2], axis=-1, keepdims=True)
    b = jnp.sum(lp[2:3] * lp[3:4], axis=-1, keepdims=True)
    return jnp.exp(a) - jnp.exp(b) + lam_init


def _diff_finish(o1, o2, lam, sub_g, lam_init):
    dlt = o1 - lam * o2
    return _rms(dlt, sub_g) * (1.0 - lam_init)


def _online_step(heads, kv_k, v_t, bias, carry):
    return _online_update(_dot_nt(kv_k, heads), v_t, bias, carry)


def _online_update(s, v_t, bias, carry):
    m_prev, l_prev, acc_prev = carry
    if bias is not None:
        if jnp.ndim(bias) == 2:
            bias = jnp.concatenate([bias] * (s.shape[1] // bias.shape[1]), axis=1)
        s = s + bias
    m_next = jnp.maximum(m_prev, jnp.max(s, axis=0, keepdims=True))
    p = jnp.exp(s - m_next)
    alpha = jnp.exp(m_prev - m_next)
    return (m_next, alpha * l_prev + jnp.sum(p, axis=0, keepdims=True),
            alpha * acc_prev + _dot(v_t, p.astype(BF16)))


def _online_init(n_heads, tq):
    w = n_heads * tq
    return (jnp.full((1, w), NEG, F32), jnp.zeros((1, w), F32), jnp.zeros((LANES, w), F32))


def _online_result(carry, h):
    _, l, acc = carry
    tq = ATTN_BLOCK
    return acc[:, h * tq:(h + 1) * tq] / jnp.maximum(l[:, h * tq:(h + 1) * tq], TINY)


def _attend_bias(ok):
    return jnp.where(ok, 0.0, NEG)


def _stacked_positions(first, rows):
    r = lax.broadcasted_iota(jnp.int32, (4 * rows, 1), 0)
    return first + (r & (rows - 1))


def _causal_t(tq):
    return lax.broadcasted_iota(jnp.int32, (tq, tq), 0) <= lax.broadcasted_iota(jnp.int32, (tq, tq), 1)


def _diff_attn_kernel(lam_ref, subt_ref, q_ref, k_ref, vt_ref, o_ref, *, lam_init):
    i = pl.program_id(2)
    tq = q_ref.shape[0]
    q = q_ref[...]
    qa, qb = q[:, :LANES], q[:, LANES:]
    heads = jnp.concatenate(
        [_low_half(qa), _low_half(qb), _high_half(qa), _high_half(qb)], axis=0).astype(BF16)

    def step(j, bias, carry):
        keys = k_ref[pl.ds(pl.multiple_of(j * tq, tq), tq), :]
        return _online_step(heads, keys, vt_ref[j], bias, carry)

    pairs = lax.shift_right_logical(i, 1)
    carry = lax.fori_loop(0, pairs, lambda p, c: step(2 * p + 1, None, step(2 * p, None, c)),
                          _online_init(4, tq))
    carry = lax.fori_loop(2 * pairs, i, lambda j, c: step(j, None, c), carry)
    carry = step(i, _attend_bias(_causal_t(tq)), carry)
    lam = _diff_lambda(lam_ref, lam_init)
    for r in range(DIFF_REP):
        dlt = _online_result(carry, r) - lam * _online_result(carry, DIFF_REP + r)
        ms = jnp.mean(dlt * dlt, axis=0, keepdims=True)
        res = dlt * lax.rsqrt(ms + EPS) * subt_ref[...] * (1.0 - lam_init)
        o_ref[:, r * LANES:(r + 1) * LANES] = res.T.astype(o_ref.dtype)


def _diff_attn(q, kvb, vt, lam_p, sub_g, lam_init, batch):
    m, nq = q.shape
    tq = ATTN_BLOCK
    t = m // batch
    nt = t // tq
    gw = nq // DIFF_KV_HEADS
    return pl.pallas_call(
        functools.partial(_diff_attn_kernel, lam_init=lam_init),
        grid=(batch, DIFF_KV_HEADS, nt),
        in_specs=[
            pl.BlockSpec((4, HEAD_DIM), lambda b, g, i: (0, 0)),
            pl.BlockSpec((2 * HEAD_DIM, 1), lambda b, g, i: (0, 0)),
            pl.BlockSpec((tq, gw), lambda b, g, i: (b * nt + i, g)),
            pl.BlockSpec((t, LANES), lambda b, g, i: (b, 2 * g)),
            pl.BlockSpec((nt, LANES, tq), lambda b, g, i: (b, g, 0)),
        ],
        out_specs=pl.BlockSpec((tq, gw), lambda b, g, i: (b * nt + i, g)),
        out_shape=jax.ShapeDtypeStruct((m, nq), BF16),
        compiler_params=_params("arbitrary", "arbitrary", "arbitrary"),
        name="diff_attn",
    )(lam_p, sub_g.reshape(-1, 1), q, kvb, vt)


GROUPS = 4
ROWS_PER_KEY = 8


def _interleaved_pages(cache):
    nl, n_p, s, g, w = cache.shape
    x = cache.reshape(nl, n_p, s, g, w // LANES, LANES).transpose(0, 1, 2, 4, 3, 5)
    return x.reshape(nl, n_p, s * (w // LANES) * g, LANES)


def _rows_per_group(x, width, offset):
    parts = []
    for g in range(GROUPS):
        piece = x[:, g * width + offset:g * width + offset + LANES]
        parts.append(jnp.concatenate([piece] * 4, axis=0))
    return jnp.concatenate(parts, axis=0)


def _diff_sample_kernel(pt_ref, lam_ref, sub_ref, q_ref, kvn_ref, *refs, lam_init, n_pages):
    del pt_ref
    pages = refs[:n_pages]
    o_ref = refs[n_pages]
    lam = _diff_lambda(lam_ref, lam_init)
    sub_g = sub_ref[...]
    gw = 4 * HEAD_DIM
    rows = []
    for g in range(GROUPS):
        qg = q_ref[:, g * gw:(g + 1) * gw]
        qa, qb = qg[:, :LANES], qg[:, LANES:]
        rows += [_low_half(qa), _low_half(qb), _high_half(qa), _high_half(qb)]
    q16 = jnp.concatenate(rows, axis=0).astype(BF16)
    new = kvn_ref[...].astype(BF16).astype(F32)
    k_new = _rows_per_group(new, gw, 0)
    v_new = _rows_per_group(new, gw, LANES)
    xs = [pages[p][...].astype(BF16) for p in range(n_pages)]
    psz = xs[0].shape[0]
    s = jnp.concatenate([_dot_nt(q16, x) for x in xs], axis=1)
    row_g = lax.shift_right_logical(lax.broadcasted_iota(jnp.int32, (4 * GROUPS, 1), 0), 2)
    col = lax.broadcasted_iota(jnp.int32, (1, s.shape[1]), 1)
    own = (col & (ROWS_PER_KEY - 1)) == row_g
    s = jnp.where(own, s, NEG)
    s_new = jnp.sum(q16.astype(F32) * k_new, axis=-1, keepdims=True)
    mx = jnp.maximum(jnp.max(s, axis=-1, keepdims=True), s_new)
    p = jnp.where(own, jnp.exp(s - mx), 0.0)
    p_new = jnp.exp(s_new - mx)
    den = jnp.sum(p, axis=-1, keepdims=True) + p_new
    acc = p_new.astype(BF16).astype(F32) * v_new
    for pi, x in enumerate(xs):
        pv = pltpu.roll(p[:, pi * psz:(pi + 1) * psz], GROUPS, 1).astype(BF16)
        acc = acc + _dot(pv, x)
    o = acc / jnp.maximum(den, TINY)
    for g in range(GROUPS):
        for r in range(DIFF_REP):
            a = 4 * g + r
            res = _diff_finish(o[a:a + 1], o[a + 2:a + 3], lam, sub_g, lam_init)
            o_ref[:, g * gw + r * LANES:g * gw + (r + 1) * LANES] = res.astype(o_ref.dtype)


def _page_spec(shape_tail, layer, p):
    return pl.BlockSpec((None, None) + shape_tail, lambda b, pt: (layer, pt[b, p], 0, 0))


def _diff_sample(q, kv_new, cache, layer, page_table, lam_p, sub_g, lam_init):
    m, nq = q.shape
    n_pages = page_table.shape[1]
    cache = _interleaved_pages(cache)
    tail = cache.shape[2:]
    row = lambda b, pt: (b, 0, 0)
    grid_spec = pltpu.PrefetchScalarGridSpec(
        num_scalar_prefetch=1,
        grid=(m,),
        in_specs=[
            pl.BlockSpec((4, HEAD_DIM), lambda b, pt: (0, 0)),
            pl.BlockSpec((1, 2 * HEAD_DIM), lambda b, pt: (0, 0)),
            pl.BlockSpec((None, 1, nq), row),
            pl.BlockSpec((None, 1, kv_new.shape[1]), row),
        ] + [_page_spec(tail, layer, p) for p in range(n_pages)],
        out_specs=pl.BlockSpec((None, 1, nq), row),
    )
    out = pl.pallas_call(
        functools.partial(_diff_sample_kernel, lam_init=lam_init, n_pages=n_pages),
        grid_spec=grid_spec,
        out_shape=jax.ShapeDtypeStruct((m, 1, nq), BF16),
        compiler_params=_params("arbitrary"),
        name="diff_sample",
    )(page_table, lam_p, sub_g, q[:, None, :], kv_new[:, None, :], *([cache] * n_pages))
    return out[:, 0, :]


def _nsa_proj_kernel(x_ref, g_ref, w_ref, wc_ref, wt_ref, q_ref, rows_ref, rowsb_ref, win_ref, winb_ref,
                     gates_ref, ce_ref, co_ref, kvt_ref):
    tm = x_ref.shape[0]
    h = _rms(x_ref[...], g_ref[...]).astype(BF16)
    p = _dot(h, w_ref[...])
    if kvt_ref is not None:
        _store_transposed(wt_ref, h, kvt_ref)
    nq, nr, nw = q_ref.shape[1], rowsb_ref.shape[1], winb_ref.shape[1]
    q_ref[...] = p[:, :nq] * QK_SCALE
    rows = p[:, nq:nq + nr]
    win = p[:, nq + nr:nq + nr + nw]
    if kvt_ref is not None:
        _store_grouped(rows, rows_ref)
        _store_grouped(win, win_ref)
    else:
        rows_ref[...] = rows
        win_ref[...] = win
    rowsb_ref[...] = rows.astype(BF16)
    winb_ref[...] = win.astype(BF16)
    gl = p[:, nq + nr + nw:]
    gates_ref[...] = 1.0 / (1.0 + jnp.exp(-gl))
    if ce_ref is not None:
        wc = wc_ref[...]
        gw = nr // NSA_GROUPS
        for g in range(NSA_GROUPS):
            kc = rows[:, g * gw:g * gw + LANES].reshape(tm // SEL_BLOCK, SEL_BLOCK, LANES) * wc[None]
            ce_ref[:, g * LANES:(g + 1) * LANES] = jnp.sum(kc[:, :CMP_BLOCK], axis=1)
            co_ref[:, g * LANES:(g + 1) * LANES] = jnp.sum(kc[:, CMP_BLOCK:], axis=1)


def _nsa_proj_kernel_nocmp(x_ref, g_ref, w_ref, q_ref, rows_ref, rowsb_ref, win_ref, winb_ref, gates_ref):
    _nsa_proj_kernel(x_ref, g_ref, w_ref, None, None, q_ref, rows_ref, rowsb_ref, win_ref, winb_ref,
                     gates_ref, None, None, None)


_NSA_NQ = NSA_GROUPS * NSA_HPG * HEAD_DIM
_NSA_NR = NSA_GROUPS * 4 * HEAD_DIM
_NSA_NW = NSA_GROUPS * 2 * HEAD_DIM
_NSA_NG = NSA_GROUPS * LANES


def _nsa_proj(x, g, w, wc, wt, tm, compress):
    m, d = x.shape
    n = w.shape[1]
    row = lambda i: (i, 0)
    in_specs = [pl.BlockSpec((tm, d), row), _resident((1, d)), _resident((d, n))]
    k = _NSA_NR // LANES if compress else 1
    kw = _NSA_NW // LANES if compress else 1
    outs = [(1, _NSA_NQ, F32), (k, _NSA_NR // k, F32), (1, _NSA_NR, BF16), (kw, _NSA_NW // kw, F32),
            (1, _NSA_NW, BF16), (1, _NSA_NG, F32)]
    out_specs = [pl.BlockSpec((tm * r, wd), row) for r, wd, _ in outs]
    out_shape = [jax.ShapeDtypeStruct((m * r, wd), dt) for r, wd, dt in outs]
    args = [x, g, w]
    kern = _nsa_proj_kernel_nocmp
    if compress:
        nb = tm // SEL_BLOCK
        in_specs += [_resident((SEL_BLOCK, LANES)), _resident(wt.shape)]
        args += [wc, wt]
        out_specs += [pl.BlockSpec((nb, NSA_GROUPS * LANES), row)] * 2
        out_shape += [jax.ShapeDtypeStruct((m // SEL_BLOCK, NSA_GROUPS * LANES), F32)] * 2
        out_specs.append(pl.BlockSpec((tm // ATTN_BLOCK, wt.shape[0], ATTN_BLOCK), lambda i: (i, 0, 0)))
        out_shape.append(jax.ShapeDtypeStruct((m // ATTN_BLOCK, wt.shape[0], ATTN_BLOCK), BF16))
        kern = _nsa_proj_kernel
    return pl.pallas_call(
        kern,
        grid=(m // tm,),
        in_specs=in_specs,
        out_specs=out_specs,
        out_shape=out_shape,
        compiler_params=_params("arbitrary"),
        name="nsa_proj",
    )(*args)


def _nsa_queries(q):
    qa, qb = q[:, :LANES], q[:, LANES:]
    parts = [_low_half(qa), _low_half(_swap_halves(qa)), _low_half(qb), _low_half(_swap_halves(qb))]
    return jnp.concatenate(parts, axis=0).astype(BF16)


def _select_blocks(imp, n_keep):
    n = imp.shape[1]
    sidx = lax.broadcasted_iota(jnp.int32, imp.shape, 1)
    cnt = jnp.zeros(imp.shape, F32)
    for s2 in range(n):
        col = imp[:, s2:s2 + 1]
        beats = (col > imp) | ((col == imp) & (sidx > s2))
        cnt = cnt + jnp.where(beats, 1.0, 0.0)
    return jnp.where(cnt < n_keep, 1.0, 0.0)


def _head_rows(x, rows):
    return [x[h * rows:(h + 1) * rows] for h in range(NSA_HPG)]


def _nsa_gate_mix(gates, o_c, o_s, o_w):
    mixed = []
    for h in range(NSA_HPG):
        mixed.append(gates[:, 3 * h:3 * h + 1] * o_c[h] + gates[:, 3 * h + 1:3 * h + 2] * o_s[h]
                     + gates[:, 3 * h + 2:3 * h + 3] * o_w[h])
    lane = lax.broadcasted_iota(jnp.int32, mixed[0].shape, 1)
    pair0 = jnp.where(lane < HEAD_DIM, _swap_halves(mixed[0]), mixed[1])
    pair1 = jnp.where(lane < HEAD_DIM, _swap_halves(mixed[2]), mixed[3])
    return pair0, pair1


def _select_blocks_t(imp_t, n_keep):
    n = imp_t.shape[0]
    sidx = lax.broadcasted_iota(jnp.int32, imp_t.shape, 0)
    cnt = jnp.zeros(imp_t.shape, F32)
    for s2 in range(n):
        row = imp_t[s2:s2 + 1, :]
        beats = (row > imp_t) | ((row == imp_t) & (sidx > s2))
        cnt = cnt + jnp.where(beats, 1.0, 0.0)
    return jnp.where(cnt < n_keep, 1.0, 0.0)


def _nsa_attn_kernel(q_ref, gates_ref, ce_ref, co_ref, rows_ref, win_ref, selt_ref, wint_ref, emat_ref, o_ref):
    i = pl.program_id(2)
    tq = q_ref.shape[0]
    qs = _nsa_queries(q_ref[...])
    tpos4 = _stacked_positions(i * tq, tq)
    kpos0 = lax.broadcasted_iota(jnp.int32, (1, tq), 1)

    n_half = ce_ref.shape[0]
    kvc = jnp.concatenate([ce_ref[...], co_ref[...]], axis=0).astype(BF16)
    cidx = lax.broadcasted_iota(jnp.int32, (1, 2 * n_half), 1)
    cblk = jnp.where(cidx < n_half, 2 * cidx, 2 * (cidx - n_half) + 1)
    mask_c = ((cblk + 1) * CMP_BLOCK - 1) <= tpos4
    p_c = _masked_softmax(_dot_nt(qs, kvc), mask_c)
    o_c = _dot(p_c.astype(BF16), kvc)

    tpos_t = i * tq + kpos0
    tpos4_t = jnp.concatenate([tpos_t] * 4, axis=1)
    ridx = lax.broadcasted_iota(jnp.int32, (2 * n_half, 1), 0)
    rblk = jnp.where(ridx < n_half, 2 * ridx, 2 * (ridx - n_half) + 1)
    mask_t = ((rblk + 1) * CMP_BLOCK - 1) <= tpos4_t
    s_t = jnp.where(mask_t, _dot_nt(kvc, qs), NEG)
    p_t = jnp.where(mask_t, jnp.exp(s_t - jnp.max(s_t, axis=0, keepdims=True)), 0.0)
    p_t = p_t / jnp.maximum(jnp.sum(p_t, axis=0, keepdims=True), TINY)
    psum = p_t[:, 0:tq] + p_t[:, tq:2 * tq] + p_t[:, 2 * tq:3 * tq] + p_t[:, 3 * tq:]
    imp = psum[:n_half] + psum[n_half:]
    sidx = lax.broadcasted_iota(jnp.int32, (n_half, 1), 0)
    forced = (sidx == lax.shift_right_logical(tpos_t, int(math.log2(SEL_BLOCK)))) | (sidx == 0)
    valid = sidx * SEL_BLOCK <= tpos_t
    imp = jnp.where(forced, FORCE_SCORE, jnp.where(valid, imp, NEG))
    sel_t = _select_blocks_t(imp, min(N_SEL, n_half)).astype(BF16)

    heads = qs
    causal = _causal_t(tq)

    def transposed(carry):
        return [_online_result(carry, h).T for h in range(NSA_HPG)]

    def sel_step(j, extra_ok, carry):
        keys = rows_ref[pl.ds(pl.multiple_of(j * tq, tq), tq), LANES:]
        ok = _dot(emat_ref[j], sel_t) > 0.5
        if extra_ok is not None:
            ok = ok & extra_ok
        return _online_step(heads, keys, selt_ref[j], _attend_bias(ok), carry)

    pairs = lax.shift_right_logical(i, 1)
    carry = lax.fori_loop(0, pairs, lambda p, c: sel_step(2 * p + 1, None, sel_step(2 * p, None, c)),
                          _online_init(NSA_HPG, tq))
    carry = lax.fori_loop(2 * pairs, i, lambda j, c: sel_step(j, None, c), carry)
    o_s = transposed(sel_step(i, causal, carry))

    n_back = WINDOW // tq
    carry = _online_init(NSA_HPG, tq)
    for dj in range(n_back, -1, -1):
        j = i - dj
        jc = jnp.maximum(j, 0)
        keys = win_ref[pl.ds(pl.multiple_of(jc * tq, tq), tq), :]
        if dj == n_back:
            kpos = j * tq + lax.broadcasted_iota(jnp.int32, (tq, tq), 0)
            qpos = i * tq + lax.broadcasted_iota(jnp.int32, (tq, tq), 1)
            bias = _attend_bias((kpos > qpos - WINDOW) & (kpos >= 0))
        elif dj == 0:
            bias = _attend_bias(causal)
        else:
            bias = jnp.where(j >= 0, 0.0, NEG)
        carry = _online_step(heads, keys, wint_ref[jc], bias, carry)
    o_w = transposed(carry)

    pair0, pair1 = _nsa_gate_mix(gates_ref[...], _head_rows(o_c, tq), o_s, o_w)
    o_ref[:, :LANES] = pair0.astype(o_ref.dtype)
    o_ref[:, LANES:] = pair1.astype(o_ref.dtype)


def _sel_expand_matrix_t(n_blocks, n_keys, tk):
    key = lax.broadcasted_iota(jnp.int32, (n_keys, n_blocks), 0)
    blk = lax.broadcasted_iota(jnp.int32, (n_keys, n_blocks), 1)
    e = jnp.where(key // SEL_BLOCK == blk, 1.0, 0.0).astype(BF16)
    return e.reshape(n_keys // tk, tk, n_blocks)


def _nsa_attn(q, gates, ce, co, rowsb, winb, kvt, batch):
    m, nq = q.shape
    tq = ATTN_BLOCK
    t = m // batch
    nt = t // tq
    n_half = t // SEL_BLOCK
    gw = nq // NSA_GROUPS
    assert WINDOW % tq == 0
    emat = _sel_expand_matrix_t(n_half, t, tq)
    return pl.pallas_call(
        _nsa_attn_kernel,
        grid=(batch, NSA_GROUPS, nt),
        in_specs=[
            pl.BlockSpec((tq, gw), lambda b, g, i: (b * nt + i, g)),
            pl.BlockSpec((tq, LANES), lambda b, g, i: (b * nt + i, g)),
            pl.BlockSpec((n_half, LANES), lambda b, g, i: (b, g)),
            pl.BlockSpec((n_half, LANES), lambda b, g, i: (b, g)),
            pl.BlockSpec((t, gw), lambda b, g, i: (b, g)),
            pl.BlockSpec((t, LANES), lambda b, g, i: (b, g)),
            pl.BlockSpec((nt, LANES, tq), lambda b, g, i: (b, g, 0)),
            pl.BlockSpec((nt, LANES, tq), lambda b, g, i: (b, NSA_GROUPS + g, 0)),
            _resident((nt, tq, n_half)),
        ],
        out_specs=pl.BlockSpec((tq, gw), lambda b, g, i: (b * nt + i, g)),
        out_shape=jax.ShapeDtypeStruct((m, nq), BF16),
        compiler_params=_params("arbitrary", "arbitrary", "arbitrary"),
        name="nsa_attn",
    )(q, gates, ce, co, rowsb, winb, kvt, kvt, emat)


def _softmax_with_new(s, ok, s_new):
    s = jnp.where(ok, s, NEG)
    mx = jnp.maximum(jnp.max(s, axis=-1, keepdims=True), s_new)
    p = jnp.where(ok, jnp.exp(s - mx), 0.0)
    p_new = jnp.exp(s_new - mx)
    return p, p_new, jnp.sum(p, axis=-1, keepdims=True) + p_new


def _nsa_sample_kernel(pt_ref, q_ref, gates_ref, rown_ref, winn_ref, wrep_ref, wst_ref, *refs, n_pages):
    del pt_ref
    pages = refs[:n_pages]
    o_ref, wout_ref, cmp_ref = refs[n_pages:n_pages + 3]
    gw = 4 * HEAD_DIM
    psz = pages[0].shape[0]
    n_wrows = wst_ref.shape[0]
    n_old = n_wrows // GROUPS
    n_half = cmp_ref.shape[0]
    q16 = jnp.concatenate([_nsa_queries(q_ref[:, g * gw:(g + 1) * gw]) for g in range(GROUPS)], axis=0)
    q16f = q16.astype(F32)
    row_g = lax.shift_right_logical(lax.broadcasted_iota(jnp.int32, (4 * GROUPS, 1), 0), 2)
    xs = [pages[p][...] for p in range(n_pages)]

    wrep = wrep_ref[...]
    nb = psz // ROWS_PER_KEY // CMP_BLOCK
    for p, x in enumerate(xs):
        blocks = jnp.sum((x * wrep).reshape(nb, CMP_BLOCK, ROWS_PER_KEY, LANES), axis=1)
        cmp_ref[p * nb // 2:(p + 1) * nb // 2] = blocks.reshape(nb // 2, 2, ROWS_PER_KEY, LANES)

    o_c, sel16 = [], []
    for g in range(GROUPS):
        q8 = jnp.concatenate([q16[4 * g:4 * g + 4]] * 2, axis=0)
        kvc = jnp.concatenate([cmp_ref[:, 0, g, :], cmp_ref[:, 1, g, :]], axis=0).astype(BF16)
        s_c = _dot_nt(q8, kvc)
        p_c = _masked_softmax(s_c, jnp.full(s_c.shape, True))
        o_c.append(_dot(p_c.astype(BF16), kvc)[0:4])
        psum = p_c[0:1] + p_c[1:2] + p_c[2:3] + p_c[3:4]
        imp = psum[:, :n_half] + psum[:, n_half:]
        sidx = lax.broadcasted_iota(jnp.int32, (1, n_half), 1)
        imp = jnp.where(sidx == 0, FORCE_SCORE, imp)
        sel16.append(_select_blocks(jnp.concatenate([imp] * 4, axis=0), N_SEL - 1))
    o_c = jnp.concatenate(o_c, axis=0)
    sel16 = jnp.concatenate(sel16, axis=0)

    xb = [x.astype(BF16) for x in xs]
    s_s = jnp.concatenate([_dot_nt(q16, x) for x in xb], axis=1)
    col = lax.broadcasted_iota(jnp.int32, (1, psz), 1)
    own = (col & (ROWS_PER_KEY - 1)) == (row_g + GROUPS)
    first_half = col < psz // 2
    ok = []
    for p in range(n_pages):
        chosen = jnp.where(first_half, sel16[:, 2 * p:2 * p + 1], sel16[:, 2 * p + 1:2 * p + 2]) > 0.5
        ok.append(own & chosen)
    new_rows = rown_ref[...].astype(BF16).astype(F32)
    ks_new = _rows_per_group(new_rows, gw, LANES)
    s_new = jnp.sum(q16f * ks_new, axis=-1, keepdims=True)
    p_s, p_new, den = _softmax_with_new(s_s, jnp.concatenate(ok, axis=1), s_new)
    acc = p_new.astype(BF16).astype(F32) * ks_new
    for pi, x in enumerate(xb):
        acc = acc + _dot(p_s[:, pi * psz:(pi + 1) * psz].astype(BF16), x)
    o_s = acc / jnp.maximum(den, TINY)

    wb = wst_ref[...].astype(BF16)
    wcol = lax.broadcasted_iota(jnp.int32, (1, n_wrows), 1)
    ok_w = ((wcol & (GROUPS - 1)) == row_g) & (lax.shift_right_logical(wcol, 2) > n_old - WINDOW)
    w_new = _rows_per_group(winn_ref[...].astype(BF16).astype(F32), LANES, 0)
    s_wn = jnp.sum(q16f * w_new, axis=-1, keepdims=True)
    p_w, p_wn, den = _softmax_with_new(_dot_nt(q16, wb), ok_w, s_wn)
    acc = p_wn.astype(BF16).astype(F32) * w_new + _dot(p_w.astype(BF16), wb)
    o_w = acc / jnp.maximum(den, TINY)

    for g in range(GROUPS):
        sl = slice(4 * g, 4 * g + 4)
        pair0, pair1 = _nsa_gate_mix(gates_ref[:, g * LANES:(g + 1) * LANES], _head_rows(o_c[sl], 1),
                                     _head_rows(o_s[sl], 1), _head_rows(o_w[sl], 1))
        o_ref[:, g * gw:g * gw + LANES] = pair0.astype(o_ref.dtype)
        o_ref[:, g * gw + LANES:(g + 1) * gw] = pair1.astype(o_ref.dtype)

    wout_ref[0:n_wrows - GROUPS, :] = wst_ref[GROUPS:n_wrows, :]
    wout_ref[n_wrows - GROUPS:n_wrows, :] = jnp.concatenate(
        [winn_ref[:, g * LANES:(g + 1) * LANES] for g in range(GROUPS)], axis=0)


def _nsa_sample(q, gates, rows_new, win_new, wc32, cache, win_state, layer, page_table):
    m, nq = q.shape
    n_pages = page_table.shape[1]
    cache = _interleaved_pages(cache)
    tail = cache.shape[2:]
    keys_per_page = tail[0] // ROWS_PER_KEY
    n_cmp = n_pages * keys_per_page // CMP_BLOCK
    nl, _, n_old, wg, ww = win_state.shape
    win_rows = win_state.reshape(nl, m, n_old * wg, ww)
    wrep = jnp.repeat(jnp.tile(wc32, (keys_per_page // CMP_BLOCK, 1)), ROWS_PER_KEY, axis=0)
    row = lambda b, pt: (b, 0, 0)
    grid_spec = pltpu.PrefetchScalarGridSpec(
        num_scalar_prefetch=1,
        grid=(m,),
        in_specs=[
            pl.BlockSpec((None, 1, nq), row),
            pl.BlockSpec((None, 1, gates.shape[1]), row),
            pl.BlockSpec((None, 1, rows_new.shape[1]), row),
            pl.BlockSpec((None, 1, win_new.shape[1]), row),
            pl.BlockSpec(wrep.shape, lambda b, pt: (0, 0)),
            pl.BlockSpec((None, None, n_old * wg, ww), lambda b, pt: (layer, b, 0, 0)),
        ] + [_page_spec(tail, layer, p) for p in range(n_pages)],
        out_specs=[
            pl.BlockSpec((None, 1, nq), row),
            pl.BlockSpec((None, n_old * wg, ww), row),
        ],
        scratch_shapes=[pltpu.VMEM((n_cmp // 2, 2, ROWS_PER_KEY, LANES), F32)],
    )
    o, wout = pl.pallas_call(
        functools.partial(_nsa_sample_kernel, n_pages=n_pages),
        grid_spec=grid_spec,
        out_shape=[
            jax.ShapeDtypeStruct((m, 1, nq), BF16),
            jax.ShapeDtypeStruct((m, n_old * wg, ww), F32),
        ],
        compiler_params=_params("arbitrary"),
        name="nsa_sample",
    )(page_table, q[:, None, :], gates[:, None, :], rows_new[:, None, :], win_new[:, None, :], wrep,
      win_rows, *([cache] * n_pages))
    return o[:, 0, :], wout.reshape(m, n_old, wg, ww)


def _nsa_pair_weights_t(w):
    pair = 2 * HEAD_DIM
    sel = [w[:, _NSA_NQ + g * 2 * pair + pair:_NSA_NQ + (g + 1) * 2 * pair] for g in range(NSA_GROUPS)]
    win = w[:, _NSA_NQ + _NSA_NR:_NSA_NQ + _NSA_NR + _NSA_NW]
    return jnp.concatenate(sel + [win], axis=1).T


def _nsa_weight_layout(w_in):
    d = w_in.shape[0]
    kv0 = _NSA_NQ
    per_g = 6 * HEAD_DIM
    g0 = kv0 + NSA_GROUPS * per_g
    rows_idx = [kv0 + g * per_g + c for g in range(NSA_GROUPS) for c in range(4 * HEAD_DIM)]
    win_idx = [kv0 + g * per_g + 4 * HEAD_DIM + c for g in range(NSA_GROUPS) for c in range(2 * HEAD_DIM)]
    n_gate = NSA_HPG * 3
    gate_cols = []
    for g in range(NSA_GROUPS):
        gate_cols.append(w_in[:, g0 + g * n_gate:g0 + (g + 1) * n_gate])
        gate_cols.append(jnp.zeros((d, LANES - n_gate), w_in.dtype))
    return jnp.concatenate(
        [w_in[:, :kv0], w_in[:, jnp.array(rows_idx)], w_in[:, jnp.array(win_idx)]] + gate_cols, axis=1)


def kernel(x_prompt, x_sample, state_conv, cache_nsa, state_nsa_win, cache_diff, page_table, norm_g, ffn_w_in,
           ffn_w_out, conv_w_in, conv_w, conv_w_out, nsa_w_in, nsa_w_cmp, nsa_w_out, diff_w_in, diff_lambda,
           diff_subln_g, diff_w_out):
    batch, seq, d = x_prompt.shape
    dec_batch, dec_seq, _ = x_sample.shape
    assert dec_seq == 1
    depth = norm_g.shape[0]
    xp = x_prompt.reshape(batch * seq, d)
    xs = x_sample.reshape(dec_batch, d)
    tm = 512
    ffn_in = ffn_w_in.astype(BF16)
    ffn_out = ffn_w_out.astype(BF16)
    outs = {k: [] for k in ("conv_p", "conv_s", "nsa_p", "nsa_s", "win_p", "win_s", "diff_p", "diff_s")}
    for i in range(depth):
        g = [norm_g[i, k][None, :] for k in range(6)]
        kind, j = i % N_MIXERS, i // N_MIXERS
        xp = _ffn(xp, g[0], g[1], ffn_in, ffn_out, i, 0, tm)
        xs = _ffn(xs, g[0], g[1], ffn_in, ffn_out, i, 0, dec_batch)
        if kind == 0:
            w_in, w_out = conv_w_in[j].astype(BF16), conv_w_out[j].astype(BF16)
            xp, st = _conv_prompt(xp, g[2], g[3], w_in, conv_w[j], w_out, batch, tm)
            outs["conv_p"].append(st)
            xs, st = _conv_sample(xs, state_conv[j].reshape(dec_batch, -1), g[2], g[3], w_in, conv_w[j], w_out)
            outs["conv_s"].append(st.reshape(dec_batch, CONV_W - 1, d))
        elif kind == 1:
            w_in = _nsa_weight_layout(nsa_w_in[j]).astype(BF16)
            w_out = nsa_w_out[j].astype(BF16)
            wc32 = jnp.concatenate([nsa_w_cmp[j, 0], nsa_w_cmp[j, 1]], axis=1)
            wc = jnp.concatenate([wc32, wc32], axis=0)
            wt = _nsa_pair_weights_t(w_in)
            q, rows, rowsb, win, winb, gates, ce, co, kvt = _nsa_proj(xp, g[2], w_in, wc, wt, tm, True)
            o = _nsa_attn(q, gates, ce, co, rowsb, winb, kvt, batch)
            xp = _post(xp, o, w_out, g[3], tm)
            outs["nsa_p"].append(_grouped_view(rows, (batch, seq), 4 * HEAD_DIM))
            wkeep = min(WINDOW, seq)
            outs["win_p"].append(_grouped_view(win, (batch, seq), 2 * HEAD_DIM)[:, seq - wkeep:])
            q, rows, _, win, _, gates = _nsa_proj(xs, g[2], w_in, None, None, dec_batch, False)
            o, wout = _nsa_sample(q, gates, rows, win, wc32, cache_nsa, state_nsa_win, j, page_table)
            xs = _post(xs, o, w_out, g[3], dec_batch)
            outs["nsa_s"].append(rows.reshape(dec_batch, 1, NSA_GROUPS, 4 * HEAD_DIM))
            outs["win_s"].append(wout)
        else:
            lam_init = 0.8 - 0.6 * math.exp(-0.3 * i)
            w_in, w_out = diff_w_in[j].astype(BF16), diff_w_out[j].astype(BF16)
            sub_g = diff_subln_g[j][None, :]
            q, kv, kvb, vt = _diff_proj(xp, g[2], w_in, tm, _diff_value_weights_t(diff_w_in[j]))
            o = _diff_attn(q, kvb, vt, diff_lambda[j], sub_g, lam_init, batch)
            xp = _post(xp, o, w_out, g[3], tm)
            outs["diff_p"].append(_grouped_view(kv, (batch, seq), 4 * HEAD_DIM))
            q, kv, _ = _diff_proj(xs, g[2], w_in, dec_batch)
            o = _diff_sample(q, kv, cache_diff, j, page_table, diff_lambda[j], sub_g, lam_init)
            xs = _post(xs, o, w_out, g[3], dec_batch)
            outs["diff_s"].append(kv.reshape(dec_batch, 1, DIFF_KV_HEADS, 4 * HEAD_DIM))
        xp = _ffn(xp, g[4], g[5], ffn_in, ffn_out, i, 1, tm)
        xs = _ffn(xs, g[4], g[5], ffn_in, ffn_out, i, 1, dec_batch)
    st = lambda k: jnp.stack(outs[k])
    return (xp.reshape(batch, seq, d), xs.reshape(dec_batch, 1, d), st("conv_p"), st("conv_s"), st("nsa_p"),
            st("nsa_s"), st("win_p"), st("win_s"), st("diff_p"), st("diff_s"))
```

```python
import functools
import math

import jax
import jax.numpy as jnp
from jax import lax
from jax.experimental import pallas as pl
from jax.experimental.pallas import tpu as pltpu

F32 = jnp.float32
BF16 = jnp.bfloat16

EPS = 1e-6
NEG = -1e30
TINY = 1e-30
N_MIXERS = 3
CONV_W = 3
HEAD_DIM = 64
NSA_GROUPS = 4
NSA_HPG = 4
CMP_BLOCK = 32
SEL_BLOCK = 64
N_SEL = 8
WINDOW = 512
FORCE_SCORE = 1e4
DIFF_KV_HEADS = 4
DIFF_REP = 2
QK_SCALE = HEAD_DIM ** -0.5 * math.log2(math.e)

LANES = 128
VMEM_LIMIT_BYTES = 56 * 1024 * 1024

_NT = (((1,), (1,)), ((), ()))


def _params(*sem):
    return pltpu.CompilerParams(dimension_semantics=sem, vmem_limit_bytes=VMEM_LIMIT_BYTES)


def _rms(x, g):
    return x * lax.rsqrt(jnp.mean(x * x, axis=-1, keepdims=True) + EPS) * g


def _dot(a, b):
    return jnp.dot(a, b, preferred_element_type=F32)


def _dot_nt(a, b):
    return lax.dot_general(a, b, _NT, preferred_element_type=F32)


def _masked_softmax(s, mask):
    s = jnp.where(mask, s, NEG)
    m = jnp.max(s, axis=-1, keepdims=True)
    p = jnp.where(mask, jnp.exp2(s - m), 0.0)
    return p / jnp.maximum(jnp.sum(p, axis=-1, keepdims=True), TINY)


def _low_half(x):
    lane = lax.broadcasted_iota(jnp.int32, x.shape, 1)
    return jnp.where(lane < HEAD_DIM, x, 0.0)


def _high_half(x):
    lane = lax.broadcasted_iota(jnp.int32, x.shape, 1)
    return jnp.where(lane >= HEAD_DIM, x, 0.0)


def _swap_halves(x):
    return pltpu.roll(x, HEAD_DIM, 1)


def _ffn_kernel(x_ref, gpre_ref, gpost_ref, wi_ref, wo_ref, o_ref):
    f = wo_ref.shape[0]
    x = x_ref[...]
    xn = _rms(x, gpre_ref[...]).astype(BF16)
    gate = _dot(xn, wi_ref[:, :f])
    up = _dot(xn, wi_ref[:, f:])
    act = (gate * (1.0 / (1.0 + jnp.exp(-gate))) * up).astype(BF16)
    o_ref[...] = x + 0.5 * _rms(_dot(act, wo_ref[...]), gpost_ref[...])


def _resident(shape):
    return pl.BlockSpec(shape, lambda *_: (0,) * len(shape), pipeline_mode=pl.Buffered(1))


def _ffn(x, g_pre, g_post, w_in, w_out, layer, half, tm):
    m, d = x.shape

    def picked(w):
        return pl.BlockSpec((None, None) + w.shape[2:], lambda i: (layer, half, 0, 0),
                            pipeline_mode=pl.Buffered(1))

    return pl.pallas_call(
        _ffn_kernel,
        grid=(m // tm,),
        in_specs=[
            pl.BlockSpec((tm, d), lambda i: (i, 0)),
            _resident((1, d)),
            _resident((1, d)),
            picked(w_in),
            picked(w_out),
        ],
        out_specs=pl.BlockSpec((tm, d), lambda i: (i, 0)),
        out_shape=jax.ShapeDtypeStruct((m, d), F32),
        compiler_params=_params("arbitrary"),
        name="half_ffn",
    )(x, g_pre, g_post, w_in, w_out)


def _post_kernel(x_ref, o_ref, w_ref, g_ref, y_ref):
    y = _dot(o_ref[...], w_ref[...])
    y_ref[...] = x_ref[...] + _rms(y, g_ref[...])


def _post(x, o, w, g, tm):
    m, d = x.shape
    k = o.shape[1]
    return pl.pallas_call(
        _post_kernel,
        grid=(m // tm,),
        in_specs=[
            pl.BlockSpec((tm, d), lambda i: (i, 0)),
            pl.BlockSpec((tm, k), lambda i: (i, 0)),
            pl.BlockSpec((k, d), lambda i: (0, 0)),
            pl.BlockSpec((1, d), lambda i: (0, 0)),
        ],
        out_specs=pl.BlockSpec((tm, d), lambda i: (i, 0)),
        out_shape=jax.ShapeDtypeStruct((m, d), F32),
        compiler_params=_params("arbitrary"),
        name="mixer_out",
    )(x, o, w, g)


def _conv_prompt_kernel(x_ref, g2_ref, g3_ref, win_ref, cw_ref, wout_ref, y_ref, st_ref, ubuf_ref):
    t = pl.program_id(1)
    tm, d = x_ref.shape
    x = x_ref[...]
    h = _rms(x, g2_ref[...]).astype(BF16)
    p = _dot(h, win_ref[...])
    c, b, v = p[:, :d], p[:, d:2 * d], p[:, 2 * d:]
    u = c * v

    @pl.when(t == 0)
    def _():
        ubuf_ref[0:8, :] = jnp.zeros((8, d), F32)

    ubuf_ref[8:8 + tm, :] = u
    cw = cw_ref[...]
    y = cw[0:1] * ubuf_ref[6:6 + tm, :] + cw[1:2] * ubuf_ref[7:7 + tm, :] + cw[2:3] * u
    out = _dot((b * y).astype(BF16), wout_ref[...])
    y_ref[...] = x + _rms(out, g3_ref[...])
    last2 = ubuf_ref[6 + tm:8 + tm, :]
    st_ref[...] = last2
    ubuf_ref[6:8, :] = last2


def _conv_prompt(x, g2, g3, w_in, cw, w_out, batch, tm):
    m, d = x.shape
    t = m // batch
    nt = t // tm
    return pl.pallas_call(
        _conv_prompt_kernel,
        grid=(batch, nt),
        in_specs=[
            pl.BlockSpec((tm, d), lambda b, i: (b * nt + i, 0)),
            pl.BlockSpec((1, d), lambda b, i: (0, 0)),
            pl.BlockSpec((1, d), lambda b, i: (0, 0)),
            pl.BlockSpec((d, 3 * d), lambda b, i: (0, 0)),
            pl.BlockSpec((CONV_W, d), lambda b, i: (0, 0)),
            pl.BlockSpec((d, d), lambda b, i: (0, 0)),
        ],
        out_specs=[
            pl.BlockSpec((tm, d), lambda b, i: (b * nt + i, 0)),
            pl.BlockSpec((None, CONV_W - 1, d), lambda b, i: (b, 0, 0)),
        ],
        out_shape=[
            jax.ShapeDtypeStruct((m, d), F32),
            jax.ShapeDtypeStruct((batch, CONV_W - 1, d), F32),
        ],
        scratch_shapes=[pltpu.VMEM((tm + 8, d), F32)],
        compiler_params=_params("arbitrary", "arbitrary"),
        name="conv_prompt",
    )(x, g2, g3, w_in, cw, w_out)


def _conv_sample_kernel(x_ref, st_ref, g2_ref, g3_ref, win_ref, cw_ref, wout_ref, y_ref, nst_ref):
    d = x_ref.shape[1]
    x = x_ref[...]
    h = _rms(x, g2_ref[...]).astype(BF16)
    p = _dot(h, win_ref[...])
    c, b, v = p[:, :d], p[:, d:2 * d], p[:, 2 * d:]
    u = c * v
    s0, s1 = st_ref[:, :d], st_ref[:, d:]
    cw = cw_ref[...]
    y = cw[0:1] * s0 + cw[1:2] * s1 + cw[2:3] * u
    out = _dot((b * y).astype(BF16), wout_ref[...])
    y_ref[...] = x + _rms(out, g3_ref[...])
    nst_ref[:, :d] = s1
    nst_ref[:, d:] = u


def _conv_sample(x, state, g2, g3, w_in, cw, w_out):
    m, d = x.shape
    return pl.pallas_call(
        _conv_sample_kernel,
        out_shape=[
            jax.ShapeDtypeStruct((m, d), F32),
            jax.ShapeDtypeStruct((m, (CONV_W - 1) * d), F32),
        ],
        compiler_params=pltpu.CompilerParams(vmem_limit_bytes=VMEM_LIMIT_BYTES),
        name="conv_sample",
    )(x, state, g2, g3, w_in, cw, w_out)


ATTN_BLOCK = 256
TOKEN_TILE = 512


def _store_grouped(x, out_ref):
    rows = x.shape[0]
    w = x.shape[1] // GROUPS
    nc = w // LANES
    for g in range(GROUPS):
        for c in range(nc):
            out_ref[pl.ds(c * GROUPS + g, rows, stride=nc * GROUPS), :] = (
                x[:, g * w + c * LANES:g * w + (c + 1) * LANES])


def _grouped_view(o, lead, w):
    nc = w // LANES
    o = o.reshape(*lead, nc, GROUPS, LANES)
    return jnp.swapaxes(o, -3, -2).reshape(*lead, GROUPS, w)


def _store_transposed(wt_ref, h, out_ref):
    res = _dot_nt(wt_ref[...], h).astype(out_ref.dtype)
    for c in range(out_ref.shape[0]):
        out_ref[c] = res[:, c * ATTN_BLOCK:(c + 1) * ATTN_BLOCK]


def _diff_proj_kernel(x_ref, g_ref, w_ref, *refs):
    h = _rms(x_ref[...], g_ref[...]).astype(BF16)
    p = _dot(h, w_ref[...])
    prompt = len(refs) == 5
    if prompt:
        wvt_ref, q_ref, kv_ref, kvb_ref, vt_ref = refs
        _store_transposed(wvt_ref, h, vt_ref)
    else:
        q_ref, kv_ref, kvb_ref = refs
    nq = q_ref.shape[1]
    q_ref[...] = p[:, :nq] * QK_SCALE
    kv = p[:, nq:]
    if prompt:
        _store_grouped(kv, kv_ref)
    else:
        kv_ref[...] = kv
    kvb_ref[...] = kv.astype(BF16)


def _diff_proj(x, g, w, tm, wvt=None):
    m, d = x.shape
    n = w.shape[1]
    nq = DIFF_KV_HEADS * DIFF_REP * 2 * HEAD_DIM
    nkv = n - nq
    row = lambda i: (i, 0)
    in_specs = [pl.BlockSpec((tm, d), row), _resident((1, d)), _resident((d, n))]
    kv_rows = nkv // LANES if wvt is not None else 1
    out_specs = [pl.BlockSpec((tm, nq), row), pl.BlockSpec((tm * kv_rows, nkv // kv_rows), row),
                 pl.BlockSpec((tm, nkv), row)]
    out_shape = [jax.ShapeDtypeStruct((m, nq), F32), jax.ShapeDtypeStruct((m * kv_rows, nkv // kv_rows), F32),
                 jax.ShapeDtypeStruct((m, nkv), BF16)]
    args = [x, g, w]
    if wvt is not None:
        nb = tm // ATTN_BLOCK
        in_specs.append(_resident(wvt.shape))
        args.append(wvt)
        out_specs.append(pl.BlockSpec((nb, wvt.shape[0], ATTN_BLOCK), lambda i: (i, 0, 0)))
        out_shape.append(jax.ShapeDtypeStruct((m // ATTN_BLOCK, wvt.shape[0], ATTN_BLOCK), BF16))
    return pl.pallas_call(
        _diff_proj_kernel,
        grid=(m // tm,),
        in_specs=in_specs,
        out_specs=out_specs,
        out_shape=out_shape,
        compiler_params=_params("arbitrary"),
        name="diff_proj",
    )(*args)


def _diff_value_weights_t(w_in):
    nq = DIFF_KV_HEADS * DIFF_REP * 2 * HEAD_DIM
    kv = w_in[:, nq:].reshape(w_in.shape[0], DIFF_KV_HEADS, 4 * HEAD_DIM)
    v = kv[:, :, 2 * HEAD_DIM:].reshape(w_in.shape[0], DIFF_KV_HEADS * 2 * HEAD_DIM)
    return v.T.astype(BF16)


def _diff_lambda(lam_ref, lam_init):
    lp = lam_ref[...]
    a = jnp.sum(lp[0:1] * lp[1:2], axis=-1, keepdims=True)
    b = jnp.sum(lp[2:3] * lp[3:4], axis=-1, keepdims=True)
    return jnp.exp(a) - jnp.exp(b) + lam_init


def _diff_finish(o1, o2, lam, sub_g, lam_init):
    dlt = o1 - lam * o2
    return _rms(dlt, sub_g) * (1.0 - lam_init)


def _online_step(heads, kv_k, v_t, bias, carry):
    return _online_update(_dot_nt(kv_k, heads), v_t, bias, carry)


def _online_update(s, v_t, bias, carry):
    m_prev, l_prev, acc_prev = carry
    if bias is not None:
        if jnp.ndim(bias) == 2:
            bias = jnp.concatenate([bias] * (s.shape[1] // bias.shape[1]), axis=1)
        s = s + bias
    m_next = jnp.maximum(m_prev, jnp.max(s, axis=0, keepdims=True))
    p = jnp.exp2(s - m_next)
    alpha = jnp.exp2(m_prev - m_next)
    return (m_next, alpha * l_prev + jnp.sum(p, axis=0, keepdims=True),
            alpha * acc_prev + _dot(v_t, p.astype(BF16)))


def _online_init(n_heads, tq):
    w = n_heads * tq
    return (jnp.full((1, w), NEG, F32), jnp.zeros((1, w), F32), jnp.zeros((LANES, w), F32))


def _online_result(carry, h):
    _, l, acc = carry
    tq = ATTN_BLOCK
    return acc[:, h * tq:(h + 1) * tq] / jnp.maximum(l[:, h * tq:(h + 1) * tq], TINY)


def _attend_bias(ok):
    return jnp.where(ok, 0.0, NEG)


def _stacked_positions(first, rows):
    r = lax.broadcasted_iota(jnp.int32, (4 * rows, 1), 0)
    return first + (r & (rows - 1))


def _causal_t(tq):
    return lax.broadcasted_iota(jnp.int32, (tq, tq), 0) <= lax.broadcasted_iota(jnp.int32, (tq, tq), 1)


def _diff_attn_kernel(lam_ref, subt_ref, q_ref, k_ref, vt_ref, o_ref, *, lam_init):
    i = pl.program_id(2)
    tq = q_ref.shape[0]
    q = q_ref[...]
    qa, qb = q[:, :LANES], q[:, LANES:]
    heads = jnp.concatenate(
        [_low_half(qa), _low_half(qb), _high_half(qa), _high_half(qb)], axis=0).astype(BF16)

    def step(j, bias, carry):
        keys = k_ref[pl.ds(pl.multiple_of(j * tq, tq), tq), :]
        return _online_step(heads, keys, vt_ref[j], bias, carry)

    pairs = lax.shift_right_logical(i, 1)
    carry = lax.fori_loop(0, pairs, lambda p, c: step(2 * p + 1, None, step(2 * p, None, c)),
                          _online_init(4, tq))
    carry = lax.fori_loop(2 * pairs, i, lambda j, c: step(j, None, c), carry)
    carry = step(i, _attend_bias(_causal_t(tq)), carry)
    lam = _diff_lambda(lam_ref, lam_init)
    for r in range(DIFF_REP):
        dlt = _online_result(carry, r) - lam * _online_result(carry, DIFF_REP + r)
        ms = jnp.mean(dlt * dlt, axis=0, keepdims=True)
        res = dlt * lax.rsqrt(ms + EPS) * subt_ref[...] * (1.0 - lam_init)
        o_ref[:, r * LANES:(r + 1) * LANES] = res.T.astype(o_ref.dtype)


def _diff_attn(q, kvb, vt, lam_p, sub_g, lam_init, batch):
    m, nq = q.shape
    tq = ATTN_BLOCK
    t = m // batch
    nt = t // tq
    gw = nq // DIFF_KV_HEADS
    return pl.pallas_call(
        functools.partial(_diff_attn_kernel, lam_init=lam_init),
        grid=(batch, DIFF_KV_HEADS, nt),
        in_specs=[
            pl.BlockSpec((4, HEAD_DIM), lambda b, g, i: (0, 0)),
            pl.BlockSpec((2 * HEAD_DIM, 1), lambda b, g, i: (0, 0)),
            pl.BlockSpec((tq, gw), lambda b, g, i: (b * nt + i, g)),
            pl.BlockSpec((t, LANES), lambda b, g, i: (b, 2 * g)),
            pl.BlockSpec((nt, LANES, tq), lambda b, g, i: (b, g, 0)),
        ],
        out_specs=pl.BlockSpec((tq, gw), lambda b, g, i: (b * nt + i, g)),
        out_shape=jax.ShapeDtypeStruct((m, nq), BF16),
        compiler_params=_params("arbitrary", "arbitrary", "arbitrary"),
        name="diff_attn",
    )(lam_p, sub_g.reshape(-1, 1), q, kvb, vt)


GROUPS = 4
ROWS_PER_KEY = 8


def _interleaved_pages(cache):
    nl, n_p, s, g, w = cache.shape
    x = cache.reshape(nl, n_p, s, g, w // LANES, LANES).transpose(0, 1, 2, 4, 3, 5)
    return x.reshape(nl, n_p, s * (w // LANES) * g, LANES)


def _rows_per_group(x, width, offset):
    parts = []
    for g in range(GROUPS):
        piece = x[:, g * width + offset:g * width + offset + LANES]
        parts.append(jnp.concatenate([piece] * 4, axis=0))
    return jnp.concatenate(parts, axis=0)


def _diff_sample_kernel(pt_ref, lam_ref, sub_ref, q_ref, kvn_ref, *refs, lam_init, n_pages):
    del pt_ref
    pages = refs[:n_pages]
    o_ref = refs[n_pages]
    lam = _diff_lambda(lam_ref, lam_init)
    sub_g = sub_ref[...]
    gw = 4 * HEAD_DIM
    rows = []
    for g in range(GROUPS):
        qg = q_ref[:, g * gw:(g + 1) * gw]
        qa, qb = qg[:, :LANES], qg[:, LANES:]
        rows += [_low_half(qa), _low_half(qb), _high_half(qa), _high_half(qb)]
    q16 = jnp.concatenate(rows, axis=0).astype(BF16)
    new = kvn_ref[...].astype(BF16).astype(F32)
    k_new = _rows_per_group(new, gw, 0)
    v_new = _rows_per_group(new, gw, LANES)
    xs = [pages[p][...].astype(BF16) for p in range(n_pages)]
    psz = xs[0].shape[0]
    s = jnp.concatenate([_dot_nt(q16, x) for x in xs], axis=1)
    row_g = lax.shift_right_logical(lax.broadcasted_iota(jnp.int32, (4 * GROUPS, 1), 0), 2)
    col = lax.broadcasted_iota(jnp.int32, (1, s.shape[1]), 1)
    own = (col & (ROWS_PER_KEY - 1)) == row_g
    s = jnp.where(own, s, NEG)
    s_new = jnp.sum(q16.astype(F32) * k_new, axis=-1, keepdims=True)
    mx = jnp.maximum(jnp.max(s, axis=-1, keepdims=True), s_new)
    p = jnp.where(own, jnp.exp2(s - mx), 0.0)
    p_new = jnp.exp2(s_new - mx)
    den = jnp.sum(p, axis=-1, keepdims=True) + p_new
    acc = p_new.astype(BF16).astype(F32) * v_new
    for pi, x in enumerate(xs):
        pv = pltpu.roll(p[:, pi * psz:(pi + 1) * psz], GROUPS, 1).astype(BF16)
        acc = acc + _dot(pv, x)
    o = acc / jnp.maximum(den, TINY)
    for g in range(GROUPS):
        for r in range(DIFF_REP):
            a = 4 * g + r
            res = _diff_finish(o[a:a + 1], o[a + 2:a + 3], lam, sub_g, lam_init)
            o_ref[:, g * gw + r * LANES:g * gw + (r + 1) * LANES] = res.astype(o_ref.dtype)


def _page_spec(shape_tail, layer, p):
    return pl.BlockSpec((None, None) + shape_tail, lambda b, pt: (layer, pt[b, p], 0, 0))


def _diff_sample(q, kv_new, cache, layer, page_table, lam_p, sub_g, lam_init):
    m, nq = q.shape
    n_pages = page_table.shape[1]
    cache = _interleaved_pages(cache)
    tail = cache.shape[2:]
    row = lambda b, pt: (b, 0, 0)
    grid_spec = pltpu.PrefetchScalarGridSpec(
        num_scalar_prefetch=1,
        grid=(m,),
        in_specs=[
            pl.BlockSpec((4, HEAD_DIM), lambda b, pt: (0, 0)),
            pl.BlockSpec((1, 2 * HEAD_DIM), lambda b, pt: (0, 0)),
            pl.BlockSpec((None, 1, nq), row),
            pl.BlockSpec((None, 1, kv_new.shape[1]), row),
        ] + [_page_spec(tail, layer, p) for p in range(n_pages)],
        out_specs=pl.BlockSpec((None, 1, nq), row),
    )
    out = pl.pallas_call(
        functools.partial(_diff_sample_kernel, lam_init=lam_init, n_pages=n_pages),
        grid_spec=grid_spec,
        out_shape=jax.ShapeDtypeStruct((m, 1, nq), BF16),
        compiler_params=_params("arbitrary"),
        name="diff_sample",
    )(page_table, lam_p, sub_g, q[:, None, :], kv_new[:, None, :], *([cache] * n_pages))
    return out[:, 0, :]


def _nsa_proj_kernel(x_ref, g_ref, w_ref, wc_ref, wt_ref, q_ref, rows_ref, rowsb_ref, win_ref, winb_ref,
                     gates_ref, ce_ref, co_ref, kvt_ref):
    tm = x_ref.shape[0]
    h = _rms(x_ref[...], g_ref[...]).astype(BF16)
    p = _dot(h, w_ref[...])
    if kvt_ref is not None:
        _store_transposed(wt_ref, h, kvt_ref)
    nq, nr, nw = q_ref.shape[1], rowsb_ref.shape[1], winb_ref.shape[1]
    q_ref[...] = p[:, :nq] * QK_SCALE
    rows = p[:, nq:nq + nr]
    win = p[:, nq + nr:nq + nr + nw]
    if kvt_ref is not None:
        _store_grouped(rows, rows_ref)
        _store_grouped(win, win_ref)
    else:
        rows_ref[...] = rows
        win_ref[...] = win
    rowsb_ref[...] = rows.astype(BF16)
    winb_ref[...] = win.astype(BF16)
    gl = p[:, nq + nr + nw:]
    gates_ref[...] = 1.0 / (1.0 + jnp.exp(-gl))
    if ce_ref is not None:
        wc = wc_ref[...]
        gw = nr // NSA_GROUPS
        for g in range(NSA_GROUPS):
            kc = rows[:, g * gw:g * gw + LANES].reshape(tm // SEL_BLOCK, SEL_BLOCK, LANES) * wc[None]
            ce_ref[:, g * LANES:(g + 1) * LANES] = jnp.sum(kc[:, :CMP_BLOCK], axis=1)
            co_ref[:, g * LANES:(g + 1) * LANES] = jnp.sum(kc[:, CMP_BLOCK:], axis=1)


def _nsa_proj_kernel_nocmp(x_ref, g_ref, w_ref, q_ref, rows_ref, rowsb_ref, win_ref, winb_ref, gates_ref):
    _nsa_proj_kernel(x_ref, g_ref, w_ref, None, None, q_ref, rows_ref, rowsb_ref, win_ref, winb_ref,
                     gates_ref, None, None, None)


_NSA_NQ = NSA_GROUPS * NSA_HPG * HEAD_DIM
_NSA_NR = NSA_GROUPS * 4 * HEAD_DIM
_NSA_NW = NSA_GROUPS * 2 * HEAD_DIM
_NSA_NG = NSA_GROUPS * LANES


def _nsa_proj(x, g, w, wc, wt, tm, compress):
    m, d = x.shape
    n = w.shape[1]
    row = lambda i: (i, 0)
    in_specs = [pl.BlockSpec((tm, d), row), _resident((1, d)), _resident((d, n))]
    k = _NSA_NR // LANES if compress else 1
    kw = _NSA_NW // LANES if compress else 1
    outs = [(1, _NSA_NQ, F32), (k, _NSA_NR // k, F32), (1, _NSA_NR, BF16), (kw, _NSA_NW // kw, F32),
            (1, _NSA_NW, BF16), (1, _NSA_NG, F32)]
    out_specs = [pl.BlockSpec((tm * r, wd), row) for r, wd, _ in outs]
    out_shape = [jax.ShapeDtypeStruct((m * r, wd), dt) for r, wd, dt in outs]
    args = [x, g, w]
    kern = _nsa_proj_kernel_nocmp
    if compress:
        nb = tm // SEL_BLOCK
        in_specs += [_resident((SEL_BLOCK, LANES)), _resident(wt.shape)]
        args += [wc, wt]
        out_specs += [pl.BlockSpec((nb, NSA_GROUPS * LANES), row)] * 2
        out_shape += [jax.ShapeDtypeStruct((m // SEL_BLOCK, NSA_GROUPS * LANES), F32)] * 2
        out_specs.append(pl.BlockSpec((tm // ATTN_BLOCK, wt.shape[0], ATTN_BLOCK), lambda i: (i, 0, 0)))
        out_shape.append(jax.ShapeDtypeStruct((m // ATTN_BLOCK, wt.shape[0], ATTN_BLOCK), BF16))
        kern = _nsa_proj_kernel
    return pl.pallas_call(
        kern,
        grid=(m // tm,),
        in_specs=in_specs,
        out_specs=out_specs,
        out_shape=out_shape,
        compiler_params=_params("arbitrary"),
        name="nsa_proj",
    )(*args)


def _nsa_queries(q):
    qa, qb = q[:, :LANES], q[:, LANES:]
    parts = [_low_half(qa), _low_half(_swap_halves(qa)), _low_half(qb), _low_half(_swap_halves(qb))]
    return jnp.concatenate(parts, axis=0).astype(BF16)


def _select_blocks(imp, n_keep):
    n = imp.shape[1]
    sidx = lax.broadcasted_iota(jnp.int32, imp.shape, 1)
    cnt = jnp.zeros(imp.shape, F32)
    for s2 in range(n):
        col = imp[:, s2:s2 + 1]
        beats = (col > imp) | ((col == imp) & (sidx > s2))
        cnt = cnt + jnp.where(beats, 1.0, 0.0)
    return jnp.where(cnt < n_keep, 1.0, 0.0)


def _head_rows(x, rows):
    return [x[h * rows:(h + 1) * rows] for h in range(NSA_HPG)]


def _nsa_gate_mix(gates, o_c, o_s, o_w):
    mixed = []
    for h in range(NSA_HPG):
        mixed.append(gates[:, 3 * h:3 * h + 1] * o_c[h] + gates[:, 3 * h + 1:3 * h + 2] * o_s[h]
                     + gates[:, 3 * h + 2:3 * h + 3] * o_w[h])
    lane = lax.broadcasted_iota(jnp.int32, mixed[0].shape, 1)
    pair0 = jnp.where(lane < HEAD_DIM, _swap_halves(mixed[0]), mixed[1])
    pair1 = jnp.where(lane < HEAD_DIM, _swap_halves(mixed[2]), mixed[3])
    return pair0, pair1


def _select_blocks_t(imp_t, n_keep):
    n = imp_t.shape[0]
    sidx = lax.broadcasted_iota(jnp.int32, imp_t.shape, 0)
    cnt = jnp.zeros(imp_t.shape, F32)
    for s2 in range(n):
        row = imp_t[s2:s2 + 1, :]
        beats = (row > imp_t) | ((row == imp_t) & (sidx > s2))
        cnt = cnt + jnp.where(beats, 1.0, 0.0)
    return jnp.where(cnt < n_keep, 1.0, 0.0)


def _nsa_attn_kernel(q_ref, gates_ref, ce_ref, co_ref, rows_ref, win_ref, selt_ref, wint_ref, emat_ref, o_ref):
    i = pl.program_id(2)
    tq = q_ref.shape[0]
    qs = _nsa_queries(q_ref[...])
    tpos4 = _stacked_positions(i * tq, tq)
    kpos0 = lax.broadcasted_iota(jnp.int32, (1, tq), 1)

    n_half = ce_ref.shape[0]
    kvc = jnp.concatenate([ce_ref[...], co_ref[...]], axis=0).astype(BF16)
    cidx = lax.broadcasted_iota(jnp.int32, (1, 2 * n_half), 1)
    cblk = jnp.where(cidx < n_half, 2 * cidx, 2 * (cidx - n_half) + 1)
    mask_c = ((cblk + 1) * CMP_BLOCK - 1) <= tpos4
    p_c = _masked_softmax(_dot_nt(qs, kvc), mask_c)
    o_c = _dot(p_c.astype(BF16), kvc)

    tpos_t = i * tq + kpos0
    tpos4_t = jnp.concatenate([tpos_t] * 4, axis=1)
    ridx = lax.broadcasted_iota(jnp.int32, (2 * n_half, 1), 0)
    rblk = jnp.where(ridx < n_half, 2 * ridx, 2 * (ridx - n_half) + 1)
    mask_t = ((rblk + 1) * CMP_BLOCK - 1) <= tpos4_t
    s_t = jnp.where(mask_t, _dot_nt(kvc, qs), NEG)
    p_t = jnp.where(mask_t, jnp.exp2(s_t - jnp.max(s_t, axis=0, keepdims=True)), 0.0)
    p_t = p_t / jnp.maximum(jnp.sum(p_t, axis=0, keepdims=True), TINY)
    psum = p_t[:, 0:tq] + p_t[:, tq:2 * tq] + p_t[:, 2 * tq:3 * tq] + p_t[:, 3 * tq:]
    imp = psum[:n_half] + psum[n_half:]
    sidx = lax.broadcasted_iota(jnp.int32, (n_half, 1), 0)
    forced = (sidx == lax.shift_right_logical(tpos_t, int(math.log2(SEL_BLOCK)))) | (sidx == 0)
    valid = sidx * SEL_BLOCK <= tpos_t
    imp = jnp.where(forced, FORCE_SCORE, jnp.where(valid, imp, NEG))
    sel_t = _select_blocks_t(imp, min(N_SEL, n_half)).astype(BF16)

    heads = qs
    causal = _causal_t(tq)

    def transposed(carry):
        return [_online_result(carry, h).T for h in range(NSA_HPG)]

    def sel_step(j, extra_ok, carry):
        keys = rows_ref[pl.ds(pl.multiple_of(j * tq, tq), tq), LANES:]
        ok = _dot(emat_ref[j], sel_t) > 0.5
        if extra_ok is not None:
            ok = ok & extra_ok
        return _online_step(heads, keys, selt_ref[j], _attend_bias(ok), carry)

    pairs = lax.shift_right_logical(i, 1)
    carry = lax.fori_loop(0, pairs, lambda p, c: sel_step(2 * p + 1, None, sel_step(2 * p, None, c)),
                          _online_init(NSA_HPG, tq))
    carry = lax.fori_loop(2 * pairs, i, lambda j, c: sel_step(j, None, c), carry)
    o_s = transposed(sel_step(i, causal, carry))

    n_back = WINDOW // tq
    carry = _online_init(NSA_HPG, tq)
    for dj in range(n_back, -1, -1):
        j = i - dj
        jc = jnp.maximum(j, 0)
        keys = win_ref[pl.ds(pl.multiple_of(jc * tq, tq), tq), :]
        if dj == n_back:
            kpos = j * tq + lax.broadcasted_iota(jnp.int32, (tq, tq), 0)
            qpos = i * tq + lax.broadcasted_iota(jnp.int32, (tq, tq), 1)
            bias = _attend_bias((kpos > qpos - WINDOW) & (kpos >= 0))
        elif dj == 0:
            bias = _attend_bias(causal)
        else:
            bias = jnp.where(j >= 0, 0.0, NEG)
        carry = _online_step(heads, keys, wint_ref[jc], bias, carry)
    o_w = transposed(carry)

    pair0, pair1 = _nsa_gate_mix(gates_ref[...], _head_rows(o_c, tq), o_s, o_w)
    o_ref[:, :LANES] = pair0.astype(o_ref.dtype)
    o_ref[:, LANES:] = pair1.astype(o_ref.dtype)


def _sel_expand_matrix_t(n_blocks, n_keys, tk):
    key = lax.broadcasted_iota(jnp.int32, (n_keys, n_blocks), 0)
    blk = lax.broadcasted_iota(jnp.int32, (n_keys, n_blocks), 1)
    e = jnp.where(key // SEL_BLOCK == blk, 1.0, 0.0).astype(BF16)
    return e.reshape(n_keys // tk, tk, n_blocks)


def _nsa_attn(q, gates, ce, co, rowsb, winb, kvt, batch):
    m, nq = q.shape
    tq = ATTN_BLOCK
    t = m // batch
    nt = t // tq
    n_half = t // SEL_BLOCK
    gw = nq // NSA_GROUPS
    assert WINDOW % tq == 0
    emat = _sel_expand_matrix_t(n_half, t, tq)
    return pl.pallas_call(
        _nsa_attn_kernel,
        grid=(batch, NSA_GROUPS, nt),
        in_specs=[
            pl.BlockSpec((tq, gw), lambda b, g, i: (b * nt + i, g)),
            pl.BlockSpec((tq, LANES), lambda b, g, i: (b * nt + i, g)),
            pl.BlockSpec((n_half, LANES), lambda b, g, i: (b, g)),
            pl.BlockSpec((n_half, LANES), lambda b, g, i: (b, g)),
            pl.BlockSpec((t, gw), lambda b, g, i: (b, g)),
            pl.BlockSpec((t, LANES), lambda b, g, i: (b, g)),
            pl.BlockSpec((nt, LANES, tq), lambda b, g, i: (b, g, 0)),
            pl.BlockSpec((nt, LANES, tq), lambda b, g, i: (b, NSA_GROUPS + g, 0)),
            _resident((nt, tq, n_half)),
        ],
        out_specs=pl.BlockSpec((tq, gw), lambda b, g, i: (b * nt + i, g)),
        out_shape=jax.ShapeDtypeStruct((m, nq), BF16),
        compiler_params=_params("arbitrary", "arbitrary", "arbitrary"),
        name="nsa_attn",
    )(q, gates, ce, co, rowsb, winb, kvt, kvt, emat)


def _softmax_with_new(s, ok, s_new):
    s = jnp.where(ok, s, NEG)
    mx = jnp.maximum(jnp.max(s, axis=-1, keepdims=True), s_new)
    p = jnp.where(ok, jnp.exp2(s - mx), 0.0)
    p_new = jnp.exp2(s_new - mx)
    return p, p_new, jnp.sum(p, axis=-1, keepdims=True) + p_new


def _nsa_sample_kernel(pt_ref, q_ref, gates_ref, rown_ref, winn_ref, wrep_ref, wst_ref, *refs, n_pages):
    del pt_ref
    pages = refs[:n_pages]
    o_ref, wout_ref, cmp_ref = refs[n_pages:n_pages + 3]
    gw = 4 * HEAD_DIM
    psz = pages[0].shape[0]
    n_wrows = wst_ref.shape[0]
    n_old = n_wrows // GROUPS
    n_half = cmp_ref.shape[0]
    q16 = jnp.concatenate([_nsa_queries(q_ref[:, g * gw:(g + 1) * gw]) for g in range(GROUPS)], axis=0)
    q16f = q16.astype(F32)
    row_g = lax.shift_right_logical(lax.broadcasted_iota(jnp.int32, (4 * GROUPS, 1), 0), 2)
    xs = [pages[p][...] for p in range(n_pages)]

    wrep = wrep_ref[...]
    nb = psz // ROWS_PER_KEY // CMP_BLOCK
    for p, x in enumerate(xs):
        blocks = jnp.sum((x * wrep).reshape(nb, CMP_BLOCK, ROWS_PER_KEY, LANES), axis=1)
        cmp_ref[p * nb // 2:(p + 1) * nb // 2] = blocks.reshape(nb // 2, 2, ROWS_PER_KEY, LANES)

    o_c, sel16 = [], []
    for g in range(GROUPS):
        q8 = jnp.concatenate([q16[4 * g:4 * g + 4]] * 2, axis=0)
        kvc = jnp.concatenate([cmp_ref[:, 0, g, :], cmp_ref[:, 1, g, :]], axis=0).astype(BF16)
        s_c = _dot_nt(q8, kvc)
        p_c = _masked_softmax(s_c, jnp.full(s_c.shape, True))
        o_c.append(_dot(p_c.astype(BF16), kvc)[0:4])
        psum = p_c[0:1] + p_c[1:2] + p_c[2:3] + p_c[3:4]
        imp = psum[:, :n_half] + psum[:, n_half:]
        sidx = lax.broadcasted_iota(jnp.int32, (1, n_half), 1)
        imp = jnp.where(sidx == 0, FORCE_SCORE, imp)
        sel16.append(_select_blocks(jnp.concatenate([imp] * 4, axis=0), N_SEL - 1))
    o_c = jnp.concatenate(o_c, axis=0)
    sel16 = jnp.concatenate(sel16, axis=0)

    xb = [x.astype(BF16) for x in xs]
    s_s = jnp.concatenate([_dot_nt(q16, x) for x in xb], axis=1)
    col = lax.broadcasted_iota(jnp.int32, (1, psz), 1)
    own = (col & (ROWS_PER_KEY - 1)) == (row_g + GROUPS)
    first_half = col < psz // 2
    ok = []
    for p in range(n_pages):
        chosen = jnp.where(first_half, sel16[:, 2 * p:2 * p + 1], sel16[:, 2 * p + 1:2 * p + 2]) > 0.5
        ok.append(own & chosen)
    new_rows = rown_ref[...].astype(BF16).astype(F32)
    ks_new = _rows_per_group(new_rows, gw, LANES)
    s_new = jnp.sum(q16f * ks_new, axis=-1, keepdims=True)
    p_s, p_new, den = _softmax_with_new(s_s, jnp.concatenate(ok, axis=1), s_new)
    acc = p_new.astype(BF16).astype(F32) * ks_new
    for pi, x in enumerate(xb):
        acc = acc + _dot(p_s[:, pi * psz:(pi + 1) * psz].astype(BF16), x)
    o_s = acc / jnp.maximum(den, TINY)

    wb = wst_ref[...].astype(BF16)
    wcol = lax.broadcasted_iota(jnp.int32, (1, n_wrows), 1)
    ok_w = ((wcol & (GROUPS - 1)) == row_g) & (lax.shift_right_logical(wcol, 2) > n_old - WINDOW)
    w_new = _rows_per_group(winn_ref[...].astype(BF16).astype(F32), LANES, 0)
    s_wn = jnp.sum(q16f * w_new, axis=-1, keepdims=True)
    p_w, p_wn, den = _softmax_with_new(_dot_nt(q16, wb), ok_w, s_wn)
    acc = p_wn.astype(BF16).astype(F32) * w_new + _dot(p_w.astype(BF16), wb)
    o_w = acc / jnp.maximum(den, TINY)

    for g in range(GROUPS):
        sl = slice(4 * g, 4 * g + 4)
        pair0, pair1 = _nsa_gate_mix(gates_ref[:, g * LANES:(g + 1) * LANES], _head_rows(o_c[sl], 1),
                                     _head_rows(o_s[sl], 1), _head_rows(o_w[sl], 1))
        o_ref[:, g * gw:g * gw + LANES] = pair0.astype(o_ref.dtype)
        o_ref[:, g * gw + LANES:(g + 1) * gw] = pair1.astype(o_ref.dtype)

    wout_ref[0:n_wrows - GROUPS, :] = wst_ref[GROUPS:n_wrows, :]
    wout_ref[n_wrows - GROUPS:n_wrows, :] = jnp.concatenate(
        [winn_ref[:, g * LANES:(g + 1) * LANES] for g in range(GROUPS)], axis=0)


def _nsa_sample(q, gates, rows_new, win_new, wc32, cache, win_state, layer, page_table):
    m, nq = q.shape
    n_pages = page_table.shape[1]
    cache = _interleaved_pages(cache)
    tail = cache.shape[2:]
    keys_per_page = tail[0] // ROWS_PER_KEY
    n_cmp = n_pages * keys_per_page // CMP_BLOCK
    nl, _, n_old, wg, ww = win_state.shape
    win_rows = win_state.reshape(nl, m, n_old * wg, ww)
    wrep = jnp.repeat(jnp.tile(wc32, (keys_per_page // CMP_BLOCK, 1)), ROWS_PER_KEY, axis=0)
    row = lambda b, pt: (b, 0, 0)
    grid_spec = pltpu.PrefetchScalarGridSpec(
        num_scalar_prefetch=1,
        grid=(m,),
        in_specs=[
            pl.BlockSpec((None, 1, nq), row),
            pl.BlockSpec((None, 1, gates.shape[1]), row),
            pl.BlockSpec((None, 1, rows_new.shape[1]), row),
            pl.BlockSpec((None, 1, win_new.shape[1]), row),
            pl.BlockSpec(wrep.shape, lambda b, pt: (0, 0)),
            pl.BlockSpec((None, None, n_old * wg, ww), lambda b, pt: (layer, b, 0, 0)),
        ] + [_page_spec(tail, layer, p) for p in range(n_pages)],
        out_specs=[
            pl.BlockSpec((None, 1, nq), row),
            pl.BlockSpec((None, n_old * wg, ww), row),
        ],
        scratch_shapes=[pltpu.VMEM((n_cmp // 2, 2, ROWS_PER_KEY, LANES), F32)],
    )
    o, wout = pl.pallas_call(
        functools.partial(_nsa_sample_kernel, n_pages=n_pages),
        grid_spec=grid_spec,
        out_shape=[
            jax.ShapeDtypeStruct((m, 1, nq), BF16),
            jax.ShapeDtypeStruct((m, n_old * wg, ww), F32),
        ],
        compiler_params=_params("arbitrary"),
        name="nsa_sample",
    )(page_table, q[:, None, :], gates[:, None, :], rows_new[:, None, :], win_new[:, None, :], wrep,
      win_rows, *([cache] * n_pages))
    return o[:, 0, :], wout.reshape(m, n_old, wg, ww)


def _nsa_pair_weights_t(w):
    pair = 2 * HEAD_DIM
    sel = [w[:, _NSA_NQ + g * 2 * pair + pair:_NSA_NQ + (g + 1) * 2 * pair] for g in range(NSA_GROUPS)]
    win = w[:, _NSA_NQ + _NSA_NR:_NSA_NQ + _NSA_NR + _NSA_NW]
    return jnp.concatenate(sel + [win], axis=1).T


def _nsa_weight_layout(w_in):
    d = w_in.shape[0]
    kv0 = _NSA_NQ
    per_g = 6 * HEAD_DIM
    g0 = kv0 + NSA_GROUPS * per_g
    rows_idx = [kv0 + g * per_g + c for g in range(NSA_GROUPS) for c in range(4 * HEAD_DIM)]
    win_idx = [kv0 + g * per_g + 4 * HEAD_DIM + c for g in range(NSA_GROUPS) for c in range(2 * HEAD_DIM)]
    n_gate = NSA_HPG * 3
    gate_cols = []
    for g in range(NSA_GROUPS):
        gate_cols.append(w_in[:, g0 + g * n_gate:g0 + (g + 1) * n_gate])
        gate_cols.append(jnp.zeros((d, LANES - n_gate), w_in.dtype))
    return jnp.concatenate(
        [w_in[:, :kv0], w_in[:, jnp.array(rows_idx)], w_in[:, jnp.array(win_idx)]] + gate_cols, axis=1)


def kernel(x_prompt, x_sample, state_conv, cache_nsa, state_nsa_win, cache_diff, page_table, norm_g, ffn_w_in,
           ffn_w_out, conv_w_in, conv_w, conv_w_out, nsa_w_in, nsa_w_cmp, nsa_w_out, diff_w_in, diff_lambda,
           diff_subln_g, diff_w_out):
    batch, seq, d = x_prompt.shape
    dec_batch, dec_seq, _ = x_sample.shape
    assert dec_seq == 1
    depth = norm_g.shape[0]
    xp = x_prompt.reshape(batch * seq, d)
    xs = x_sample.reshape(dec_batch, d)
    tm = TOKEN_TILE
    ffn_in = ffn_w_in.astype(BF16)
    ffn_out = ffn_w_out.astype(BF16)
    outs = {k: [] for k in ("conv_p", "conv_s", "nsa_p", "nsa_s", "win_p", "win_s", "diff_p", "diff_s")}
    for i in range(depth):
        g = [norm_g[i, k][None, :] for k in range(6)]
        kind, j = i % N_MIXERS, i // N_MIXERS
        xp = _ffn(xp, g[0], g[1], ffn_in, ffn_out, i, 0, tm)
        xs = _ffn(xs, g[0], g[1], ffn_in, ffn_out, i, 0, dec_batch)
        if kind == 0:
            w_in, w_out = conv_w_in[j].astype(BF16), conv_w_out[j].astype(BF16)
            xp, st = _conv_prompt(xp, g[2], g[3], w_in, conv_w[j], w_out, batch, tm)
            outs["conv_p"].append(st)
            xs, st = _conv_sample(xs, state_conv[j].reshape(dec_batch, -1), g[2], g[3], w_in, conv_w[j], w_out)
            outs["conv_s"].append(st.reshape(dec_batch, CONV_W - 1, d))
        elif kind == 1:
            w_in = _nsa_weight_layout(nsa_w_in[j]).astype(BF16)
            w_out = nsa_w_out[j].astype(BF16)
            wc32 = jnp.concatenate([nsa_w_cmp[j, 0], nsa_w_cmp[j, 1]], axis=1)
            wc = jnp.concatenate([wc32, wc32], axis=0)
            wt = _nsa_pair_weights_t(w_in)
            q, rows, rowsb, win, winb, gates, ce, co, kvt = _nsa_proj(xp, g[2], w_in, wc, wt, tm, True)
            o = _nsa_attn(q, gates, ce, co, rowsb, winb, kvt, batch)
            xp = _post(xp, o, w_out, g[3], tm)
            outs["nsa_p"].append(_grouped_view(rows, (batch, seq), 4 * HEAD_DIM))
            wkeep = min(WINDOW, seq)
            outs["win_p"].append(_grouped_view(win, (batch, seq), 2 * HEAD_DIM)[:, seq - wkeep:])
            q, rows, _, win, _, gates = _nsa_proj(xs, g[2], w_in, None, None, dec_batch, False)
            o, wout = _nsa_sample(q, gates, rows, win, wc32, cache_nsa, state_nsa_win, j, page_table)
            xs = _post(xs, o, w_out, g[3], dec_batch)
            outs["nsa_s"].append(rows.reshape(dec_batch, 1, NSA_GROUPS, 4 * HEAD_DIM))
            outs["win_s"].append(wout)
        else:
            lam_init = 0.8 - 0.6 * math.exp(-0.3 * i)
            w_in, w_out = diff_w_in[j].astype(BF16), diff_w_out[j].astype(BF16)
            sub_g = diff_subln_g[j][None, :]
            q, kv, kvb, vt = _diff_proj(xp, g[2], w_in, tm, _diff_value_weights_t(diff_w_in[j]))
            o = _diff_attn(q, kvb, vt, diff_lambda[j], sub_g, lam_init, batch)
            xp = _post(xp, o, w_out, g[3], tm)
            outs["diff_p"].append(_grouped_view(kv, (batch, seq), 4 * HEAD_DIM))
            q, kv, _ = _diff_proj(xs, g[2], w_in, dec_batch)
            o = _diff_sample(q, kv, cache_diff, j, page_table, diff_lambda[j], sub_g, lam_init)
            xs = _post(xs, o, w_out, g[3], dec_batch)
            outs["diff_s"].append(kv.reshape(dec_batch, 1, DIFF_KV_HEADS, 4 * HEAD_DIM))
        xp = _ffn(xp, g[4], g[5], ffn_in, ffn_out, i, 1, tm)
        xs = _ffn(xs, g[4], g[5], ffn_in, ffn_out, i, 1, dec_batch)
    st = lambda k: jnp.stack(outs[k])
    return (xp.reshape(batch, seq, d), xs.reshape(dec_batch, 1, d), st("conv_p"), st("conv_s"), st("nsa_p"),
            st("nsa_s"), st("win_p"), st("win_s"), st("diff_p"), st("diff_s"))
```
